```python
import math
import jax, jax.numpy as jnp
from jax import lax
import numpy as np

D_MODEL = 2048
BATCH = 32
SEQ = 256
DEPTH = 1
DEC_BATCH = 4
DEC_SEQ = 4096
PAST_LEN = 512

GRID_W = 64
GLA_HEADS = 4
GLA_DK = D_MODEL // 2
GLA_DV = D_MODEL
GLA_DKH = GLA_DK // GLA_HEADS
GLA_DVH = GLA_DV // GLA_HEADS
GK_RANK = 16
GATE_NORMALIZER = 16.0
GLA_CHUNK = 32
CONV_DIM = D_MODEL
CONV_WIDTH = 3
N_EXPERTS = 32
TOP_K = 4
D_FF = D_MODEL
SWIGLU_ALPHA = 1.702
SWIGLU_LIMIT = 7.0
NORM_EPS = 1e-6
N_MOD = 6
SPLIT_SIZES = (GLA_DK, GLA_DK, GLA_DV, GLA_DV, 2 * GK_RANK, CONV_DIM, CONV_DIM, CONV_DIM, D_MODEL, D_MODEL)
IN_COLS = sum(SPLIT_SIZES)
SPLIT_POINTS = tuple(int(v) for v in np.cumsum(SPLIT_SIZES)[:-1])

kernel_name = "prefix_dit_bidir_gla_shortconv_moe_step"


def rmsnorm(x, g):
    x32 = x.astype(jnp.float32)
    y = x32 * lax.rsqrt(jnp.mean(x32 * x32, axis=-1, keepdims=True) + NORM_EPS)
    return (y * g.astype(jnp.float32)).astype(x.dtype)


def conv3(z, w, axis):
    pad = [(0, 0)] * z.ndim
    pad[axis] = (1, 1)
    zp = jnp.pad(z, pad)
    n = z.shape[axis]
    tap = lambda s: lax.slice_in_dim(zp, s, s + n, axis=axis)
    return tap(0) * w[0] + tap(1) * w[1] + tap(2) * w[2]


def conv_context(z, w):
    return conv3(z, w, 1)


def conv_latent(z, w):
    b, t, ch = z.shape
    rows = t // GRID_W
    zg = z.reshape(b, rows, GRID_W, ch)
    return conv3(zg, w, 2).reshape(b, t, ch)


def gla_chunked(q, k, v, g, s0):
    b_, t_, h_, dk = q.shape
    dv = v.shape[-1]
    n = t_ // GLA_CHUNK

    def to_chunks(a):
        return a.reshape(b_, n, GLA_CHUNK, h_, a.shape[-1]).transpose(1, 0, 3, 2, 4).astype(jnp.float32)

    mask = jnp.tril(jnp.ones((GLA_CHUNK, GLA_CHUNK), dtype=bool))[:, :, None]

    def step(S, inp):
        qc, kc, vc, gc = inp
        bcum = jnp.cumsum(gc, axis=2)
        o_inter = jnp.einsum('bhid,bhde->bhie', qc * jnp.exp(bcum), S)
        diff = bcum[:, :, :, None, :] - bcum[:, :, None, :, :]
        decay = jnp.where(mask, jnp.exp(jnp.minimum(diff, 0.0)), 0.0)
        scores = jnp.einsum('bhid,bhijd,bhjd->bhij', qc, decay, kc)
        o_intra = jnp.einsum('bhij,bhje->bhie', scores, vc)
        b_last = bcum[:, :, -1:, :]
        S = jnp.exp(b_last[:, :, 0, :])[..., None] * S + jnp.einsum(
            'bhjd,bhje->bhde', kc * jnp.exp(b_last - bcum), vc)
        return S, o_inter + o_intra

    S, o = lax.scan(step, s0.astype(jnp.float32),
                    (to_chunks(q), to_chunks(k), to_chunks(v), to_chunks(g)))
    o = o.transpose(1, 0, 3, 2, 4).reshape(b_, t_, h_, dv)
    return o, S


def gla_bidir(q, k, v, g_f, g_b, s0_f, s0_b):
    o_f, s_f = gla_chunked(q, k, v, g_f, s0_f)
    rev = lambda a: jnp.flip(a, axis=1)
    o_b, s_b = gla_chunked(rev(q), rev(k), rev(v), rev(g_b), s0_b)
    return o_f + rev(o_b), s_f, s_b


def moe(h, w_router, b_router, w1, b1, w2, b2):
    b_, t_, d = h.shape
    hf = h.reshape(b_ * t_, d)
    logits = (hf @ w_router + b_router).astype(jnp.float32)
    top_val, top_idx = lax.top_k(logits, TOP_K)
    probs = jax.nn.softmax(top_val, axis=-1)
    combine = jnp.sum(jax.nn.one_hot(top_idx, N_EXPERTS, dtype=jnp.float32) * probs[..., None],
                      axis=1).astype(h.dtype)
    out = jnp.zeros_like(hf)
    for e in range(N_EXPERTS):
        hid = hf @ w1[e] + b1[e]
        gate, up = jnp.split(hid, 2, axis=-1)
        gate = jnp.minimum(gate, SWIGLU_LIMIT)
        up = jnp.clip(up, -SWIGLU_LIMIT, SWIGLU_LIMIT)
        act = (up + 1.0) * gate * jax.nn.sigmoid(SWIGLU_ALPHA * gate)
        out = out + combine[:, e:e + 1] * (act @ w2[e] + b2[e])
    return out.reshape(b_, t_, d)


def trunk_layer(x, mod, lp, s0_f, s0_b, conv_fn):
    sh1, sc1, g1, sh2, sc2, g2 = jnp.split(mod, N_MOD, axis=-1)
    b_, t_, _ = x.shape
    h = rmsnorm(x, lp['norm1_g']) * (1.0 + sc1) + sh1
    z = h @ lp['w_in']
    q, k, v, og, gk_lo, cx, cb, cc, ga, gb = jnp.split(z, SPLIT_POINTS, axis=-1)
    q = q.reshape(b_, t_, GLA_HEADS, GLA_DKH) * (GLA_DKH ** -0.5)
    k = k.reshape(b_, t_, GLA_HEADS, GLA_DKH)
    v = v.reshape(b_, t_, GLA_HEADS, GLA_DVH)
    gk = jnp.einsum('btrk,rkd->btrd', gk_lo.reshape(b_, t_, 2, GK_RANK), lp['w_gk_up']) + lp['b_gk']
    gk = jax.nn.log_sigmoid(gk.astype(jnp.float32)) / GATE_NORMALIZER
    g_f = gk[:, :, 0].reshape(b_, t_, GLA_HEADS, GLA_DKH)
    g_b = gk[:, :, 1].reshape(b_, t_, GLA_HEADS, GLA_DKH)
    o, s_f, s_b = gla_bidir(q, k, v, g_f, g_b, s0_f, s0_b)
    o = rmsnorm(o.astype(x.dtype), lp['gla_norm_g']).reshape(b_, t_, GLA_DV) * jax.nn.silu(og)
    y_gla = o @ lp['w_gla_out']
    y_conv = (cb * conv_fn(cc * cx, lp['conv_w'])) @ lp['w_conv_out']
    merged = jax.nn.sigmoid(ga) * y_gla + jax.nn.sigmoid(gb) * y_conv
    x = x + g1 * (merged @ lp['w_o'])
    h = rmsnorm(x, lp['norm2_g']) * (1.0 + sc2) + sh2
    x = x + g2 * moe(h, lp['w_router'], lp['b_router'], lp['w_exp1'], lp['b_exp1'],
                     lp['w_exp2'], lp['b_exp2'])
    return x, s_f, s_b


def setup_inputs(seed: int = 0) -> dict:
    key = jax.random.key(seed)
    ks = jax.random.split(key, 25)
    f32 = jnp.float32
    D = D_MODEL

    def nrm(k, shape, scale):
        return jax.random.normal(k, shape, f32) * scale

    st_shape = (DEC_BATCH, DEPTH, GLA_HEADS, GLA_DKH, GLA_DVH)
    return {
        'x_prompt': nrm(ks[0], (BATCH, SEQ, D), 1.0),
        'x_sample': nrm(ks[1], (DEC_BATCH, DEC_SEQ, D), 1.0),
        'state_gla_fwd': nrm(ks[2], st_shape, 1.0),
        'state_gla_bwd': nrm(ks[3], st_shape, 1.0),
        'c': nrm(ks[4], (DEC_BATCH, D), 1.0),
        'c_ctx': nrm(ks[5], (D,), 1.0),
        'norm1_g': 1.0 + nrm(ks[6], (DEPTH, D), 0.02),
        'w_ada': nrm(ks[7], (DEPTH, D, N_MOD * D), 0.5 * D ** -0.5),
        'b_ada': nrm(ks[8], (DEPTH, N_MOD * D), 0.02),
        'w_in': nrm(ks[9], (DEPTH, D, IN_COLS), D ** -0.5),
        'w_gk_up': nrm(ks[10], (DEPTH, 2, GK_RANK, GLA_DK), GK_RANK ** -0.5),
        'b_gk': nrm(ks[11], (DEPTH, 2, GLA_DK), 0.1),
        'gla_norm_g': 1.0 + nrm(ks[12], (DEPTH, GLA_DVH), 0.02),
        'w_gla_out': nrm(ks[13], (DEPTH, GLA_DV, D), GLA_DV ** -0.5),
        'conv_w': nrm(ks[14], (DEPTH, CONV_WIDTH, CONV_DIM), CONV_WIDTH ** -0.5),
        'w_conv_out': nrm(ks[15], (DEPTH, CONV_DIM, D), CONV_DIM ** -0.5),
        'w_o': nrm(ks[16], (DEPTH, D, D), D ** -0.5),
        'norm2_g': 1.0 + nrm(ks[17], (DEPTH, D), 0.02),
        'w_router': nrm(ks[18], (DEPTH, D, N_EXPERTS), D ** -0.5),
        'b_router': nrm(ks[19], (DEPTH, N_EXPERTS), 0.01),
        'w_exp1': nrm(ks[20], (DEPTH, N_EXPERTS, D, 2 * D_FF), D ** -0.5),
        'b_exp1': nrm(ks[21], (DEPTH, N_EXPERTS, 2 * D_FF), 0.02),
        'w_exp2': nrm(ks[22], (DEPTH, N_EXPERTS, D_FF, D), D_FF ** -0.5),
        'b_exp2': nrm(ks[23], (DEPTH, N_EXPERTS, D), 0.02),
        'final_norm_g': 1.0 + nrm(ks[24], (D,), 0.02),
    }


def reference(x_prompt, x_sample, state_gla_fwd, state_gla_bwd, c, c_ctx, norm1_g, w_ada, b_ada,
              w_in, w_gk_up, b_gk, gla_norm_g, w_gla_out, conv_w, w_conv_out, w_o, norm2_g,
              w_router, b_router, w_exp1, b_exp1, w_exp2, b_exp2, final_norm_g):
    xp = x_prompt
    xs = x_sample
    zero_state = jnp.zeros((x_prompt.shape[0], GLA_HEADS, GLA_DKH, GLA_DVH), jnp.float32)
    new_f = []
    new_b = []
    for l in range(DEPTH):
        lp = {
            'norm1_g': norm1_g[l], 'w_in': w_in[l], 'w_gk_up': w_gk_up[l], 'b_gk': b_gk[l],
            'gla_norm_g': gla_norm_g[l], 'w_gla_out': w_gla_out[l], 'conv_w': conv_w[l],
            'w_conv_out': w_conv_out[l], 'w_o': w_o[l], 'norm2_g': norm2_g[l],
            'w_router': w_router[l], 'b_router': b_router[l], 'w_exp1': w_exp1[l],
            'b_exp1': b_exp1[l], 'w_exp2': w_exp2[l], 'b_exp2': b_exp2[l],
        }
        mod_ctx = (jax.nn.silu(c_ctx) @ w_ada[l] + b_ada[l])[None, None, :]
        mod_lat = (jax.nn.silu(c) @ w_ada[l] + b_ada[l])[:, None, :]
        xp, s_f, s_b = trunk_layer(xp, mod_ctx, lp, zero_state, zero_state, conv_context)
        new_f.append(s_f.astype(x_prompt.dtype))
        new_b.append(s_b.astype(x_prompt.dtype))
        xs, _, _ = trunk_layer(xs, mod_lat, lp, state_gla_fwd[:, l], state_gla_bwd[:, l], conv_latent)
    y_prompt = rmsnorm(xp, final_norm_g)
    y_sample = rmsnorm(xs, final_norm_g)
    new_state_gla_fwd = jnp.stack(new_f, axis=1)
    new_state_gla_bwd = jnp.stack(new_b, axis=1)
    return (y_prompt, y_sample, new_state_gla_fwd, new_state_gla_bwd)
```

```python
import functools

import jax
import jax.numpy as jnp
from jax import lax
from jax.experimental import pallas as pl
from jax.experimental.pallas import tpu as pltpu

F32 = jnp.float32
BF16 = jnp.bfloat16
U32 = jnp.uint32
I32 = jnp.int32

GLA_HEADS = 4
GK_RANK = 16
GATE_NORMALIZER = 16.0
TOP_K = 4
SWIGLU_ALPHA = 1.702
SWIGLU_LIMIT = 7.0
NORM_EPS = 1e-6

LANES = 128
BF16_ROWS = 16
VMEM_LIMIT = 56 * 1024 * 1024

GLA_CHUNK = 64
GLA_SUB = BF16_ROWS
NEG_BIG = -1e30

ROW_BLOCK = 1024
SUB_TILE = 256
FF_CHUNK = 256


def _cparams(sem):
    return pltpu.CompilerParams(dimension_semantics=sem, vmem_limit_bytes=VMEM_LIMIT)


def _dot(a, b):
    return jnp.dot(a, b, preferred_element_type=F32)


def _dot_nt(a, b):
    return lax.dot_general(a, b, (((1,), (1,)), ((), ())), preferred_element_type=F32)


def _dot_tn(a, b):
    return lax.dot_general(a, b, (((0,), (0,)), ((), ())), preferred_element_type=F32)


def _split_bf16(a, n):
    parts = []
    r = a
    for _ in range(n):
        p = r.astype(BF16)
        parts.append(p)
        r = r - p.astype(F32)
    return parts


def _dot_f32(a, b):
    a_hi, a_lo = _split_bf16(a, 2)
    b_hi, b_lo = _split_bf16(b, 2)
    return _dot(a_hi, b_hi) + (_dot(a_lo, b_hi) + _dot(a_hi, b_lo))


def _sigmoid(x):
    return 1.0 / (1.0 + jnp.exp(-x))


def _pack_bf16_pair(lo, hi):
    lo_b = lax.bitcast_convert_type(lo.astype(BF16).astype(F32), U32)
    hi_b = lax.bitcast_convert_type(hi.astype(BF16).astype(F32), U32)
    return (hi_b & jnp.uint32(0xFFFF0000)) | (lo_b >> 16)


def _unpack_bf16_pair(u):
    lo = lax.bitcast_convert_type(u << 16, F32)
    hi = lax.bitcast_convert_type(u & jnp.uint32(0xFFFF0000), F32)
    return lo, hi


def _ada_body(c_ref, w_ref, b_ref, o_ref):
    c = c_ref[...]
    s = c * _sigmoid(c)
    o_ref[...] = _dot(s.astype(BF16), w_ref[...].astype(BF16)) + b_ref[...]


def _ada(cond, w_ada, b_ada):
    rows, d = cond.shape
    n = w_ada.shape[1]
    bn = 1024
    return pl.pallas_call(
        _ada_body,
        grid=(n // bn,),
        in_specs=[
            pl.BlockSpec((rows, d), lambda j: (0, 0)),
            pl.BlockSpec((d, bn), lambda j: (0, j)),
            pl.BlockSpec((1, bn), lambda j: (0, j)),
        ],
        out_specs=pl.BlockSpec((rows, bn), lambda j: (0, j)),
        out_shape=jax.ShapeDtypeStruct((rows, n), F32),
        compiler_params=_cparams(("arbitrary",)),
        name="ada",
    )(cond, w_ada, b_ada.reshape(1, n))


def _pre_body(x_ref, mod_ref, g_ref, wgk_ref, h_ref, gk_ref, *, d):
    x = x_ref[...]
    ms = jnp.mean(x * x, axis=-1, keepdims=True)
    y = x * lax.rsqrt(ms + NORM_EPS) * g_ref[...]
    h = y * (1.0 + mod_ref[:, d:2 * d]) + mod_ref[:, 0:d]
    h_ref[...] = h.astype(BF16)
    gk_ref[...] = _dot_f32(h, wgk_ref[...])


def _pre(x, mod3, mod_base, tiles_per_mod, norm_g, w_gk, tm):
    m, d = x.shape
    r = w_gk.shape[1]
    return pl.pallas_call(
        functools.partial(_pre_body, d=d),
        grid=(m // tm,),
        in_specs=[
            pl.BlockSpec((tm, d), lambda i: (i, 0)),
            pl.BlockSpec((None, 1, mod3.shape[2]), lambda i: (mod_base + i // tiles_per_mod, 0, 0)),
            pl.BlockSpec((1, d), lambda i: (0, 0)),
            pl.BlockSpec((d, r), lambda i: (0, 0)),
        ],
        out_specs=[
            pl.BlockSpec((tm, d), lambda i: (i, 0)),
            pl.BlockSpec((tm, r), lambda i: (i, 0)),
        ],
        out_shape=[jax.ShapeDtypeStruct((m, d), BF16), jax.ShapeDtypeStruct((m, r), F32)],
        compiler_params=_cparams(("arbitrary",)),
        name="pre",
    )(x, mod3, norm_g.reshape(1, d), w_gk)


def _mm_body(a_ref, b_ref, o_ref):
    o_ref[...] = _dot(a_ref[...], b_ref[...]).astype(o_ref.dtype)


def _matmul(a, b, bm, bn):
    m, k = a.shape
    n = b.shape[1]
    return pl.pallas_call(
        _mm_body,
        grid=(m // bm, n // bn),
        in_specs=[
            pl.BlockSpec((bm, k), lambda i, j: (i, 0)),
            pl.BlockSpec((k, bn), lambda i, j: (0, j)),
        ],
        out_specs=pl.BlockSpec((bm, bn), lambda i, j: (i, j)),
        out_shape=jax.ShapeDtypeStruct((m, n), BF16),
        compiler_params=_cparams(("arbitrary", "arbitrary")),
        name="in_proj",
    )(a, b)


def _gla_chunk(q, k, v, gk_lo, w_up, b_up, st_ref, reverse):
    c, dk = q.shape
    gk = _dot_f32(gk_lo, w_up) + b_up
    g = (jnp.minimum(gk, 0.0) - jnp.log(1.0 + jnp.exp(-jnp.abs(gk)))) * (1.0 / GATE_NORMALIZER)
    row = lax.broadcasted_iota(I32, (c, c), 0)
    col = lax.broadcasted_iota(I32, (c, c), 1)
    tri = (col >= row) if reverse else (col <= row)
    tri_b = jnp.where(tri, 1.0, 0.0).astype(BF16)
    g1, g2, g3 = _split_bf16(g, 3)
    b = _dot(tri_b, g1) + (_dot(tri_b, g2) + _dot(tri_b, g3))

    qf = q.astype(F32) * (dk ** -0.5)
    kf = k.astype(F32)
    st = st_ref[...]
    o = _dot_nt((qf * jnp.exp(b)).astype(BF16), st.astype(BF16))

    rowv = lax.broadcasted_iota(I32, (c, 1), 0)
    lhs, rhs = [], []
    for j in range(c // GLA_SUB):
        lo, hi = j * GLA_SUB, (j + 1) * GLA_SUB
        e_row = lo if reverse else hi - 1
        e = b[e_row:e_row + 1, :]
        in_blk = (rowv >= lo) & (rowv < hi)
        active = (rowv < hi) if reverse else (rowv >= lo)
        lhs.append((qf * jnp.exp(jnp.where(active, b - e, NEG_BIG))).astype(BF16))
        rhs.append((kf * jnp.exp(jnp.where(in_blk, e - b, NEG_BIG))).astype(BF16))
    p = _dot_nt(jnp.concatenate(lhs, axis=1), jnp.concatenate(rhs, axis=1))
    p = jnp.where(tri, p, 0.0)
    o = o + _dot(p.astype(BF16), v)

    l_row = 0 if reverse else c - 1
    bl = b[l_row:l_row + 1, :]
    ks = (kf * jnp.exp(bl - b)).astype(BF16)
    st_ref[...] = st * jnp.exp(bl) + _dot_tn(v, ks)
    return o


def _gla_body(*refs, t, has_init, emit_state):
    q_ref, k_ref, v_ref, og_ref, gk_ref, wup_ref, bup_ref, gn_ref = refs[:8]
    pos = 8
    if has_init:
        s0f_ref, s0b_ref = refs[pos:pos + 2]
        pos += 2
    y_ref = refs[pos]
    pos += 1
    if emit_state:
        sf_ref, sb_ref = refs[pos:pos + 2]
        pos += 2
    o_ref, stf_ref, stb_ref = refs[pos:pos + 3]

    c = GLA_CHUNK
    n = t // c
    if has_init:
        stf_ref[...] = s0f_ref[...].T
        stb_ref[...] = s0b_ref[...].T
    else:
        stf_ref[...] = jnp.zeros_like(stf_ref)
        stb_ref[...] = jnp.zeros_like(stb_ref)
    o_ref[...] = jnp.zeros_like(o_ref)

    wup_f, wup_b = wup_ref[0], wup_ref[1]
    bup_f, bup_b = bup_ref[0], bup_ref[1]

    def step(i, carry):
        rf = pl.multiple_of(i * c, c)
        rb = pl.multiple_of((n - 1 - i) * c, c)
        of = _gla_chunk(q_ref[pl.ds(rf, c), :], k_ref[pl.ds(rf, c), :], v_ref[pl.ds(rf, c), :],
                        gk_ref[pl.ds(rf, c), :], wup_f, bup_f, stf_ref, False)
        o_ref[pl.ds(rf, c), :] += of
        ob = _gla_chunk(q_ref[pl.ds(rb, c), :], k_ref[pl.ds(rb, c), :], v_ref[pl.ds(rb, c), :],
                        gk_ref[pl.ds(rb, c), :], wup_b, bup_b, stb_ref, True)
        o_ref[pl.ds(rb, c), :] += ob
        return carry

    lax.fori_loop(0, n, step, 0)

    if emit_state:
        sf_ref[...] = stf_ref[...].T
        sb_ref[...] = stb_ref[...].T

    blk = min(t, 256)

    def fin(i, carry):
        r = pl.multiple_of(i * blk, blk)
        o = o_ref[pl.ds(r, blk), :]
        ms = jnp.mean(o * o, axis=-1, keepdims=True)
        og = og_ref[pl.ds(r, blk), :].astype(F32)
        y = o * lax.rsqrt(ms + NORM_EPS) * gn_ref[...] * (og * _sigmoid(og))
        y_ref[pl.ds(r, blk), :] = y.astype(BF16)
        return carry

    lax.fori_loop(0, t // blk, fin, 0)


def _gla(z, gk_lo, w_gk_up, b_gk, gn, n_seq, t, dk, dv, s0f=None, s0b=None, emit_state=False):
    h = GLA_HEADS
    m = n_seq * t
    has_init = s0f is not None
    qk_blocks = (h * dk) // dk
    in_specs = [
        pl.BlockSpec((t, dk), lambda s, hh: (s, hh)),
        pl.BlockSpec((t, dk), lambda s, hh: (s, qk_blocks + hh)),
        pl.BlockSpec((t, dv), lambda s, hh: (s, (2 * h * dk) // dv + hh)),
        pl.BlockSpec((t, dv), lambda s, hh: (s, (2 * h * dk) // dv + h + hh)),
        pl.BlockSpec((t, 2 * GK_RANK), lambda s, hh: (s, 0)),
        pl.BlockSpec((2, 2 * GK_RANK, dk), lambda s, hh: (0, 0, hh)),
        pl.BlockSpec((2, 1, dk), lambda s, hh: (0, 0, hh)),
        pl.BlockSpec((1, dv), lambda s, hh: (0, 0)),
    ]
    args = [z, z, z, z, gk_lo, w_gk_up, b_gk.reshape(2, 1, h * dk), gn.reshape(1, dv)]
    if has_init:
        in_specs += [pl.BlockSpec((None, dk, dv), lambda s, hh: (s * h + hh, 0, 0))] * 2
        args += [s0f.reshape(n_seq * h, dk, dv), s0b.reshape(n_seq * h, dk, dv)]
    out_specs = [pl.BlockSpec((t, dv), lambda s, hh: (s, hh))]
    out_shape = [jax.ShapeDtypeStruct((m, h * dv), BF16)]
    if emit_state:
        out_specs += [pl.BlockSpec((None, dk, dv), lambda s, hh: (s * h + hh, 0, 0))] * 2
        out_shape += [jax.ShapeDtypeStruct((n_seq * h, dk, dv), F32)] * 2
    return pl.pallas_call(
        functools.partial(_gla_body, t=t, has_init=has_init, emit_state=emit_state),
        grid=(n_seq, h),
        in_specs=in_specs,
        out_specs=out_specs,
        out_shape=out_shape,
        scratch_shapes=[pltpu.VMEM((t, dv), F32), pltpu.VMEM((dv, dk), F32), pltpu.VMEM((dv, dk), F32)],
        compiler_params=_cparams(("arbitrary", "arbitrary")),
        name="gla",
    )(*args)


def _mix_body(yg_ref, cx_ref, cb_ref, cc_ref, ga_ref, gb_ref, cw_ref, wg_ref, wc_ref, o_ref, *, seg):
    tm = yg_ref.shape[0]
    s = cc_ref[...].astype(F32) * cx_ref[...].astype(F32)
    rowv = lax.broadcasted_iota(I32, (tm, 1), 0)
    pos = rowv & (seg - 1)
    prev = jnp.where(pos == 0, 0.0, pltpu.roll(s, 1, axis=0))
    nxt = jnp.where(pos == seg - 1, 0.0, pltpu.roll(s, tm - 1, axis=0))
    conv = prev * cw_ref[0:1, :] + s * cw_ref[1:2, :] + nxt * cw_ref[2:3, :]
    u = (cb_ref[...].astype(F32) * conv).astype(BF16)
    y_gla = _dot(yg_ref[...], wg_ref[...])
    y_conv = _dot(u, wc_ref[...])
    merged = _sigmoid(ga_ref[...].astype(F32)) * y_gla + _sigmoid(gb_ref[...].astype(F32)) * y_conv
    o_ref[...] = merged.astype(BF16)


def _mix(yg, z, conv_w, w_gla_out, w_conv_out, tm, seg, d, col0):
    m = yg.shape[0]
    zspec = lambda c: pl.BlockSpec((tm, d), lambda i: (i, col0 + c))
    const = lambda shape: pl.BlockSpec(shape, lambda i: (0, 0))
    return pl.pallas_call(
        functools.partial(_mix_body, seg=seg),
        grid=(m // tm,),
        in_specs=[pl.BlockSpec((tm, d), lambda i: (i, 0)), zspec(0), zspec(1), zspec(2), zspec(3), zspec(4),
                  const(conv_w.shape), const(w_gla_out.shape), const(w_conv_out.shape)],
        out_specs=pl.BlockSpec((tm, d), lambda i: (i, 0)),
        out_shape=jax.ShapeDtypeStruct((m, d), BF16),
        compiler_params=_cparams(("arbitrary",)),
        name="mix",
    )(yg, z, z, z, z, z, conv_w, w_gla_out, w_conv_out)


def _route_body(mg_ref, x_ref, mod_ref, wo_ref, g_ref, wr_ref, br_ref, cnt_in_ref,
                x2_ref, hp_ref, route_ref, cnt_ref, carry_ref, *, d, n_exp):
    i = pl.program_id(0)
    tm = x_ref.shape[0]

    @pl.when(i == 0)
    def _():
        carry_ref[...] = cnt_in_ref[...]

    x2 = x_ref[...] + mod_ref[:, 2 * d:3 * d] * _dot(mg_ref[...], wo_ref[...])
    x2_ref[...] = x2
    ms = jnp.mean(x2 * x2, axis=-1, keepdims=True)
    h = x2 * lax.rsqrt(ms + NORM_EPS) * g_ref[...] * (1.0 + mod_ref[:, 4 * d:5 * d]) + mod_ref[:, 3 * d:4 * d]
    hp_ref[...] = _pack_bf16_pair(h[:, :d // 2], h[:, d // 2:])

    logits = _dot_f32(h, wr_ref[...]) + br_ref[...]
    lane = lax.broadcasted_iota(I32, (tm, n_exp), 1).astype(F32)
    work = logits
    vals, hots = [], []
    for _ in range(TOP_K):
        mx = jnp.max(work, axis=-1, keepdims=True)
        idx = jnp.min(jnp.where(work == mx, lane, float(n_exp)), axis=-1, keepdims=True)
        hot = lane == idx
        vals.append(mx)
        hots.append(hot)
        work = jnp.where(hot, -jnp.inf, work)
    exps = [jnp.exp(v - vals[0]) for v in vals]
    inv = 1.0 / (exps[0] + exps[1] + exps[2] + exps[3])

    cnt = jnp.zeros((tm, n_exp), F32)
    for hot in hots:
        cnt = cnt + jnp.where(hot, 1.0, 0.0)
    r = lax.broadcasted_iota(I32, (tm, tm), 0)
    cc = lax.broadcasted_iota(I32, (tm, tm), 1)
    strict = jnp.where(cc < r, 1.0, 0.0).astype(BF16)
    before = _dot(strict, cnt.astype(BF16)) + carry_ref[...]
    carry_ref[...] = carry_ref[...] + jnp.sum(cnt, axis=0, keepdims=True)

    out_lane = lax.broadcasted_iota(I32, (tm, LANES), 1)
    route = jnp.zeros((tm, LANES), I32)
    for k in range(TOP_K):
        e_k = jnp.sum(jnp.where(hots[k], lane, 0.0), axis=-1, keepdims=True).astype(I32)
        rank_k = jnp.sum(jnp.where(hots[k], before, 0.0), axis=-1, keepdims=True).astype(I32)
        p_k = lax.bitcast_convert_type(exps[k] * inv, I32)
        route = jnp.where(out_lane == k, e_k, route)
        route = jnp.where(out_lane == TOP_K + k, rank_k, route)
        route = jnp.where(out_lane == 2 * TOP_K + k, p_k, route)
    route_ref[...] = route

    @pl.when(i == pl.num_programs(0) - 1)
    def _():
        cnt_ref[...] = carry_ref[...]


def _route(merged, x, mod3, mod_base, tiles_per_mod, w_o, norm_g, w_router, b_router, counts_in, tm):
    m, d = x.shape
    n_exp = w_router.shape[1]
    const = lambda shape: pl.BlockSpec(shape, lambda i: (0, 0))
    tile = lambda w: pl.BlockSpec((tm, w), lambda i: (i, 0))
    return pl.pallas_call(
        functools.partial(_route_body, d=d, n_exp=n_exp),
        grid=(m // tm,),
        in_specs=[tile(d), tile(d),
                  pl.BlockSpec((None, 1, mod3.shape[2]), lambda i: (mod_base + i // tiles_per_mod, 0, 0)),
                  const(w_o.shape), const((1, d)), const(w_router.shape), const((1, n_exp)), const((1, n_exp))],
        out_specs=[tile(d), tile(d // 2), tile(LANES), const((1, n_exp))],
        out_shape=[jax.ShapeDtypeStruct((m, d), F32), jax.ShapeDtypeStruct((m, d // 2), U32),
                   jax.ShapeDtypeStruct((m, LANES), I32), jax.ShapeDtypeStruct((1, n_exp), F32)],
        scratch_shapes=[pltpu.VMEM((1, n_exp), F32)],
        compiler_params=_cparams(("arbitrary",)),
        name="route",
    )(merged, x, mod3, w_o, norm_g.reshape(1, d), w_router, b_router.reshape(1, n_exp), counts_in)


def _row_copy(src_ref, src_row, dst_ref, dst_row, sem):
    return pltpu.make_async_copy(src_ref.at[pl.ds(src_row, 1), :], dst_ref.at[pl.ds(dst_row, 1), :], sem)


def _dispatch_body(pos_ref, hp_ref, xs_in_ref, xs_ref, sem, *, tb):
    del xs_in_ref
    i = pl.program_id(0)
    n = tb * TOP_K

    def issue(j, carry):
        _row_copy(hp_ref, j // TOP_K, xs_ref, pos_ref[i * n + j], sem).start()
        return carry

    lax.fori_loop(0, n, issue, 0)

    def drain(j, carry):
        _row_copy(hp_ref, 0, xs_ref, 0, sem).wait()
        return carry

    lax.fori_loop(0, n, drain, 0)


def _dispatch(pos_flat, hp, xs, tb):
    m, w = hp.shape
    return pl.pallas_call(
        functools.partial(_dispatch_body, tb=tb),
        grid_spec=pltpu.PrefetchScalarGridSpec(
            num_scalar_prefetch=1,
            grid=(m // tb,),
            in_specs=[pl.BlockSpec((tb, w), lambda i, pos: (i, 0)), pl.BlockSpec(memory_space=pl.ANY)],
            out_specs=pl.BlockSpec(memory_space=pl.ANY),
            scratch_shapes=[pltpu.SemaphoreType.DMA(())],
        ),
        out_shape=jax.ShapeDtypeStruct(xs.shape, xs.dtype),
        input_output_aliases={2: 0},
        compiler_params=_cparams(("arbitrary",)),
        name="dispatch",
    )(pos_flat, hp, xs)


def _experts_body(e_ref, xb_ref, ob_ref, valid_ref, x_ref, w1g_ref, w1u_ref, b1g_ref, b1u_ref, w2_ref, b2_ref,
                  y_ref, acc_ref, xbf_ref, *, n_f):
    del e_ref, xb_ref, ob_ref
    i = pl.program_id(0)
    f = pl.program_id(1)
    valid = valid_ref[i]
    rb, d = acc_ref.shape
    n_sub = rb // SUB_TILE

    @pl.when(f == 0)
    def _():
        for s in range(n_sub):
            @pl.when(s * SUB_TILE < valid)
            def _():
                lo, hi = _unpack_bf16_pair(x_ref[s * SUB_TILE:(s + 1) * SUB_TILE, :])
                xbf_ref[s * SUB_TILE:(s + 1) * SUB_TILE, 0:d // 2] = lo.astype(BF16)
                xbf_ref[s * SUB_TILE:(s + 1) * SUB_TILE, d // 2:d] = hi.astype(BF16)

    @pl.when(valid > 0)
    def _():
        wg = w1g_ref[...].astype(BF16)
        wu = w1u_ref[...].astype(BF16)
        w2 = w2_ref[...].astype(BF16)
        for s in range(n_sub):
            @pl.when(s * SUB_TILE < valid)
            def _():
                rows = slice(s * SUB_TILE, (s + 1) * SUB_TILE)
                xs = xbf_ref[rows, :]
                gate = jnp.minimum(_dot(xs, wg) + b1g_ref[...], SWIGLU_LIMIT)
                up = jnp.clip(_dot(xs, wu) + b1u_ref[...], -SWIGLU_LIMIT, SWIGLU_LIMIT)
                act = (up + 1.0) * gate * _sigmoid(SWIGLU_ALPHA * gate)
                contrib = _dot(act.astype(BF16), w2)

                @pl.when(f == 0)
                def _():
                    acc_ref[rows, :] = contrib + b2_ref[...]

                @pl.when(f > 0)
                def _():
                    acc_ref[rows, :] += contrib

    @pl.when(f == n_f - 1)
    def _():
        for s in range(n_sub):
            rows = slice(s * SUB_TILE, (s + 1) * SUB_TILE)

            @pl.when(s * SUB_TILE < valid)
            def _():
                a = acc_ref[rows, :]
                y_ref[rows, :] = _pack_bf16_pair(a[:, :d // 2], a[:, d // 2:])

            @pl.when(s * SUB_TILE >= valid)
            def _():
                y_ref[rows, :] = jnp.zeros((SUB_TILE, d // 2), U32)


def _experts(item_e, item_xb, item_ob, item_valid, xs, w1, b1, w2, b2, n_out_blocks):
    n_items = item_e.shape[0]
    n_exp, d, ff2 = w1.shape
    ff = ff2 // 2
    n_f = ff // FF_CHUNK
    rb = ROW_BLOCK

    def fi(i, f, valid):
        return jnp.where(valid[i] > 0, f, n_f - 1)

    return pl.pallas_call(
        functools.partial(_experts_body, n_f=n_f),
        grid_spec=pltpu.PrefetchScalarGridSpec(
            num_scalar_prefetch=4,
            grid=(n_items, n_f),
            in_specs=[
                pl.BlockSpec((rb, d // 2), lambda i, f, e, xb, ob, v: (xb[i], 0)),
                pl.BlockSpec((None, d, FF_CHUNK), lambda i, f, e, xb, ob, v: (e[i], 0, fi(i, f, v))),
                pl.BlockSpec((None, d, FF_CHUNK), lambda i, f, e, xb, ob, v: (e[i], 0, n_f + fi(i, f, v))),
                pl.BlockSpec((None, 1, FF_CHUNK), lambda i, f, e, xb, ob, v: (e[i], 0, fi(i, f, v))),
                pl.BlockSpec((None, 1, FF_CHUNK), lambda i, f, e, xb, ob, v: (e[i], 0, n_f + fi(i, f, v))),
                pl.BlockSpec((None, FF_CHUNK, d), lambda i, f, e, xb, ob, v: (e[i], fi(i, f, v), 0)),
                pl.BlockSpec((None, 1, d), lambda i, f, e, xb, ob, v: (e[i], 0, 0)),
            ],
            out_specs=pl.BlockSpec((rb, d // 2), lambda i, f, e, xb, ob, v: (ob[i], 0)),
            scratch_shapes=[pltpu.VMEM((rb, d), F32), pltpu.VMEM((rb, d), BF16)],
        ),
        out_shape=jax.ShapeDtypeStruct((n_out_blocks * rb, d // 2), U32),
        compiler_params=_cparams(("arbitrary", "arbitrary")),
        name="experts",
    )(item_e, item_xb, item_ob, item_valid, xs, w1, w1, b1.reshape(n_exp, 1, ff2), b1.reshape(n_exp, 1, ff2),
      w2, b2.reshape(n_exp, 1, d))


def _combine_body(pos_ref, route_ref, x2_ref, mod_ref, g_ref, y_hbm, o_ref, buf_ref, sem, *, tb, d):
    i = pl.program_id(0)
    n = tb * TOP_K

    def issue(j, carry):
        k = j % TOP_K
        _row_copy(y_hbm, pos_ref[i * n + j], buf_ref.at[k], j // TOP_K, sem.at[k]).start()
        return carry

    lax.fori_loop(0, n, issue, 0)

    def drain(j, carry):
        k = j % TOP_K
        _row_copy(y_hbm, 0, buf_ref.at[k], 0, sem.at[k]).wait()
        return carry

    lax.fori_loop(0, n, drain, 0)

    lo = jnp.zeros((tb, d // 2), F32)
    hi = jnp.zeros((tb, d // 2), F32)
    route = route_ref[...]
    for k in range(TOP_K):
        p_k = lax.bitcast_convert_type(route[:, 2 * TOP_K + k:2 * TOP_K + k + 1], F32)
        a, b = _unpack_bf16_pair(buf_ref[k])
        lo = lo + p_k * a
        hi = hi + p_k * b
    x_lo = x2_ref[:, 0:d // 2] + mod_ref[:, 5 * d:5 * d + d // 2] * lo
    x_hi = x2_ref[:, d // 2:d] + mod_ref[:, 5 * d + d // 2:6 * d] * hi
    ms = (jnp.sum(x_lo * x_lo, axis=-1, keepdims=True) + jnp.sum(x_hi * x_hi, axis=-1, keepdims=True)) * (1.0 / d)
    inv = lax.rsqrt(ms + NORM_EPS)
    o_ref[:, 0:d // 2] = x_lo * inv * g_ref[:, 0:d // 2]
    o_ref[:, d // 2:d] = x_hi * inv * g_ref[:, d // 2:d]


def _combine(pos_flat, route, x2, mod3, mod_base, tiles_per_mod, final_g, y_sorted, tb):
    m, d = x2.shape
    return pl.pallas_call(
        functools.partial(_combine_body, tb=tb, d=d),
        grid_spec=pltpu.PrefetchScalarGridSpec(
            num_scalar_prefetch=1,
            grid=(m // tb,),
            in_specs=[
                pl.BlockSpec((tb, LANES), lambda i, pos: (i, 0)),
                pl.BlockSpec((tb, d), lambda i, pos: (i, 0)),
                pl.BlockSpec((None, 1, mod3.shape[2]), lambda i, pos: (mod_base + i // tiles_per_mod, 0, 0)),
                pl.BlockSpec((1, d), lambda i, pos: (0, 0)),
                pl.BlockSpec(memory_space=pl.ANY),
            ],
            out_specs=pl.BlockSpec((tb, d), lambda i, pos: (i, 0)),
            scratch_shapes=[pltpu.VMEM((TOP_K, tb, d // 2), U32), pltpu.SemaphoreType.DMA((TOP_K,))],
        ),
        out_shape=jax.ShapeDtypeStruct((m, d), F32),
        compiler_params=_cparams(("arbitrary",)),
        name="combine",
    )(pos_flat, route, x2, mod3, final_g.reshape(1, d), y_sorted)


def _is_pow2(v):
    return v > 0 and (v & (v - 1)) == 0


def kernel(x_prompt, x_sample, state_gla_fwd, state_gla_bwd, c, c_ctx, norm1_g, w_ada, b_ada, w_in, w_gk_up, b_gk,
           gla_norm_g, w_gla_out, conv_w, w_conv_out, w_o, norm2_g, w_router, b_router, w_exp1, b_exp1, w_exp2,
           b_exp2, final_norm_g):
    batch, seq, d = x_prompt.shape
    dec_batch, dec_seq, _ = x_sample.shape
    depth = norm1_g.shape[0]
    assert depth == 1, "single trunk layer"
    h = GLA_HEADS
    dk_all = w_gk_up.shape[-1]
    dk, dv = dk_all // h, d // h
    n_exp = w_router.shape[-1]
    grid_w = 64
    assert _is_pow2(seq) and _is_pow2(grid_w) and dec_seq % grid_w == 0
    tm = 256
    assert tm % seq == 0 and tm % grid_w == 0, "conv segments must not straddle token tiles"

    c_gk = 2 * dk_all + 2 * d
    w_in0 = w_in[0]
    w_main = jnp.concatenate([w_in0[:, :c_gk], w_in0[:, c_gk + 2 * GK_RANK:]], axis=1).astype(BF16)
    w_gk = w_in0[:, c_gk:c_gk + 2 * GK_RANK]
    zr = jnp.zeros((GK_RANK, dk_all), F32)
    w_up2 = jnp.stack([jnp.concatenate([w_gk_up[0, 0], zr], axis=0), jnp.concatenate([zr, w_gk_up[0, 1]], axis=0)])
    w_gla_out_b = w_gla_out[0].astype(BF16)
    w_conv_out_b = w_conv_out[0].astype(BF16)
    w_o_b = w_o[0].astype(BF16)

    n_cond = 1 + dec_batch
    cond = jnp.concatenate([c_ctx[None, :], c, jnp.zeros((8 - n_cond % 8, d), F32)], axis=0)
    mod = _ada(cond, w_ada[0], b_ada[0])
    mod3 = mod.reshape(mod.shape[0], 1, mod.shape[1])

    conv_col0 = (2 * dk_all + 2 * d) // d

    passes = []
    counts = jnp.zeros((1, n_exp), F32)
    for name, x3, n_seq, t, mod_base, seg in (("ctx", x_prompt, batch, seq, 0, seq),
                                              ("lat", x_sample, dec_batch, dec_seq, 1, grid_w)):
        m = n_seq * t
        x = x3.reshape(m, d)
        tiles_per_mod = (m // tm) if name == "ctx" else (t // tm)
        hb, gk_lo = _pre(x, mod3, mod_base, tiles_per_mod, norm1_g[0], w_gk, tm)
        z = _matmul(hb, w_main, min(m, 2048), 1024)
        if name == "ctx":
            yg, s_f, s_b = _gla(z, gk_lo, w_up2, b_gk[0], gla_norm_g[0], n_seq, t, dk, dv, emit_state=True)
        else:
            (yg,) = _gla(z, gk_lo, w_up2, b_gk[0], gla_norm_g[0], n_seq, t, dk, dv,
                         s0f=state_gla_fwd[:, 0], s0b=state_gla_bwd[:, 0])
            s_f = s_b = None
        merged = _mix(yg, z, conv_w[0], w_gla_out_b, w_conv_out_b, tm, seg, d, conv_col0)
        x2, hp, route, counts = _route(merged, x, mod3, mod_base, tiles_per_mod, w_o_b, norm2_g[0],
                                       w_router[0], b_router[0], counts, tm)
        passes.append(dict(x2=x2, hp=hp, route=route, s_f=s_f, s_b=s_b, mod_base=mod_base,
                           tiles_per_mod=tiles_per_mod, shape=x3.shape))

    n_tok = sum(p["x2"].shape[0] for p in passes)
    n_items = (n_tok * TOP_K) // ROW_BLOCK + n_exp
    cnt = counts[0].astype(I32)
    nblk = (cnt + ROW_BLOCK - 1) // ROW_BLOCK
    blk_end = jnp.cumsum(nblk)
    blk0 = blk_end - nblk
    total = blk_end[-1]
    item = jnp.arange(n_items, dtype=I32)
    item_e_raw = jnp.searchsorted(blk_end, item, side="right").astype(I32)
    live = item < total
    last_e = jnp.searchsorted(blk_end, total - 1, side="right").astype(I32)
    item_e = jnp.where(live, jnp.minimum(item_e_raw, n_exp - 1), last_e)
    item_xb = jnp.where(live, item, total - 1)
    item_ob = item
    item_valid = jnp.where(live, jnp.clip(cnt[item_e] - (item - blk0[item_e]) * ROW_BLOCK, 0, ROW_BLOCK), 0)
    slot0 = blk0 * ROW_BLOCK

    xs = jnp.zeros((n_items * ROW_BLOCK, d // 2), U32)
    for p in passes:
        e_idx = p["route"][:, 0:TOP_K]
        rank = p["route"][:, TOP_K:2 * TOP_K]
        p["pos"] = (slot0[e_idx] + rank).reshape(-1)
        xs = _dispatch(p["pos"], p["hp"], xs, tm)

    y_sorted = _experts(item_e, item_xb, item_ob, item_valid, xs, w_exp1[0], b_exp1[0], w_exp2[0], b_exp2[0],
                        n_items)

    outs = []
    for p in passes:
        y = _combine(p["pos"], p["route"], p["x2"], mod3, p["mod_base"], p["tiles_per_mod"], final_norm_g,
                     y_sorted, tm)
        outs.append(y.reshape(p["shape"]))

    st_shape = (batch, depth, h, dk, dv)
    return (outs[0], outs[1], passes[0]["s_f"].reshape(st_shape), passes[0]["s_b"].reshape(st_shape))
```

```python
import functools

import jax
import jax.numpy as jnp
from jax import lax
from jax.experimental import pallas as pl
from jax.experimental.pallas import tpu as pltpu

F32 = jnp.float32
BF16 = jnp.bfloat16
I32 = jnp.int32

GLA_HEADS = 4
GK_RANK = 16
GATE_NORMALIZER = 16.0
TOP_K = 4
SWIGLU_ALPHA = 1.702
SWIGLU_LIMIT = 7.0
NORM_EPS = 1e-6

LANES = 128
BF16_ROWS = 16
VMEM_LIMIT = 56 * 1024 * 1024

GLA_CHUNK = 64
GLA_SUB = BF16_ROWS
GLA_UNROLL = 4
NEG_BIG = -1e30

ROW_BLOCK = 1024
SUB_TILE = 256
FF_CHUNK = 256
OUT_CHUNK = 512


def _cparams(sem):
    return pltpu.CompilerParams(dimension_semantics=sem, vmem_limit_bytes=VMEM_LIMIT)


def _dot(a, b):
    return jnp.dot(a, b, preferred_element_type=F32)


def _dot_nt(a, b):
    return lax.dot_general(a, b, (((1,), (1,)), ((), ())), preferred_element_type=F32)


def _dot_tn(a, b):
    return lax.dot_general(a, b, (((0,), (0,)), ((), ())), preferred_element_type=F32)


def _split_bf16(a, n):
    parts = []
    r = a
    for _ in range(n):
        p = r.astype(BF16)
        parts.append(p)
        r = r - p.astype(F32)
    return parts


def _dot_f32(a, b):
    a_hi, a_lo = _split_bf16(a, 2)
    b_hi, b_lo = _split_bf16(b, 2)
    return _dot(a_hi, b_hi) + (_dot(a_lo, b_hi) + _dot(a_hi, b_lo))


def _sigmoid(x):
    return 1.0 / (1.0 + jnp.exp(-x))


def _ada_body(c_ref, w_ref, b_ref, o_ref):
    c = c_ref[...]
    s = c * _sigmoid(c)
    o_ref[...] = _dot(s.astype(BF16), w_ref[...].astype(BF16)) + b_ref[...]


def _ada(cond, w_ada, b_ada):
    rows, d = cond.shape
    n = w_ada.shape[1]
    bn = 1024
    return pl.pallas_call(
        _ada_body,
        grid=(n // bn,),
        in_specs=[
            pl.BlockSpec((rows, d), lambda j: (0, 0)),
            pl.BlockSpec((d, bn), lambda j: (0, j)),
            pl.BlockSpec((1, bn), lambda j: (0, j)),
        ],
        out_specs=pl.BlockSpec((rows, bn), lambda j: (0, j)),
        out_shape=jax.ShapeDtypeStruct((rows, n), F32),
        compiler_params=_cparams(("arbitrary",)),
        name="ada",
    )(cond, w_ada, b_ada.reshape(1, n))


def _pre_body(x_ref, mod_ref, g_ref, wgk_ref, h_ref, gk_ref, *, d):
    x = x_ref[...]
    ms = jnp.mean(x * x, axis=-1, keepdims=True)
    y = x * lax.rsqrt(ms + NORM_EPS) * g_ref[...]
    h = y * (1.0 + mod_ref[:, d:2 * d]) + mod_ref[:, 0:d]
    h_ref[...] = h.astype(BF16)
    gk_ref[...] = _dot_f32(h, wgk_ref[...])


def _pre(x, mod3, mod_base, tiles_per_mod, norm_g, w_gk, tm):
    m, d = x.shape
    r = w_gk.shape[1]
    return pl.pallas_call(
        functools.partial(_pre_body, d=d),
        grid=(m // tm,),
        in_specs=[
            pl.BlockSpec((tm, d), lambda i: (i, 0)),
            pl.BlockSpec((None, 1, mod3.shape[2]), lambda i: (mod_base + i // tiles_per_mod, 0, 0)),
            pl.BlockSpec((1, d), lambda i: (0, 0)),
            pl.BlockSpec((d, r), lambda i: (0, 0)),
        ],
        out_specs=[
            pl.BlockSpec((tm, d), lambda i: (i, 0)),
            pl.BlockSpec((tm, r), lambda i: (i, 0)),
        ],
        out_shape=[jax.ShapeDtypeStruct((m, d), BF16), jax.ShapeDtypeStruct((m, r), F32)],
        compiler_params=_cparams(("arbitrary",)),
        name="pre",
    )(x, mod3, norm_g.reshape(1, d), w_gk)


def _mm_body(a_ref, b_ref, o_ref):
    o_ref[...] = _dot(a_ref[...], b_ref[...]).astype(o_ref.dtype)


def _matmul(a, b, bm, bn):
    m, k = a.shape
    n = b.shape[1]
    return pl.pallas_call(
        _mm_body,
        grid=(m // bm, n // bn),
        in_specs=[
            pl.BlockSpec((bm, k), lambda i, j: (i, 0)),
            pl.BlockSpec((k, bn), lambda i, j: (0, j)),
        ],
        out_specs=pl.BlockSpec((bm, bn), lambda i, j: (i, j)),
        out_shape=jax.ShapeDtypeStruct((m, n), BF16),
        compiler_params=_cparams(("arbitrary", "arbitrary")),
        name="in_proj",
    )(a, b)


def _gla_prep(q, k, v, gk_lo, w_up, b_up, reverse):
    c, dk = q.shape
    gk = _dot_f32(gk_lo, w_up) + b_up
    g = (jnp.minimum(gk, 0.0) - jnp.log(1.0 + jnp.exp(-jnp.abs(gk)))) * (1.0 / GATE_NORMALIZER)
    row = lax.broadcasted_iota(I32, (c, c), 0)
    col = lax.broadcasted_iota(I32, (c, c), 1)
    tri = (col >= row) if reverse else (col <= row)
    tri_b = jnp.where(tri, 1.0, 0.0).astype(BF16)
    g1, g2, g3 = _split_bf16(g, 3)
    b = _dot(tri_b, g1) + (_dot(tri_b, g2) + _dot(tri_b, g3))

    qf = q.astype(F32) * (dk ** -0.5)
    kf = k.astype(F32)
    q_in = (qf * jnp.exp(b)).astype(BF16)

    rowv = lax.broadcasted_iota(I32, (c, 1), 0)
    lhs, rhs = [], []
    for j in range(c // GLA_SUB):
        lo, hi = j * GLA_SUB, (j + 1) * GLA_SUB
        e_row = lo if reverse else hi - 1
        e = b[e_row:e_row + 1, :]
        in_blk = (rowv >= lo) & (rowv < hi)
        active = (rowv < hi) if reverse else (rowv >= lo)
        lhs.append((qf * jnp.exp(jnp.where(active, b - e, NEG_BIG))).astype(BF16))
        rhs.append((kf * jnp.exp(jnp.where(in_blk, e - b, NEG_BIG))).astype(BF16))
    p = _dot_nt(jnp.concatenate(lhs, axis=1), jnp.concatenate(rhs, axis=1))
    p = jnp.where(tri, p, 0.0)
    o_intra = _dot(p.astype(BF16), v)

    l_row = 0 if reverse else c - 1
    bl = b[l_row:l_row + 1, :]
    ks = (kf * jnp.exp(bl - b)).astype(BF16)
    return q_in, ks, jnp.exp(bl), o_intra


def _gla_state_step(prep, v, st_ref):
    q_in, ks, decay, o_intra = prep
    st = st_ref[...]
    o = o_intra + _dot_nt(q_in, st.astype(BF16))
    st_ref[...] = st * decay + _dot_tn(v, ks)
    return o


def _gla_body(*refs, t, unroll, has_init, emit_state):
    q_ref, k_ref, v_ref, og_ref, gk_ref, wup_ref, bup_ref, gn_ref = refs[:8]
    pos = 8
    if has_init:
        s0f_ref, s0b_ref = refs[pos:pos + 2]
        pos += 2
    y_ref = refs[pos]
    pos += 1
    if emit_state:
        sf_ref, sb_ref = refs[pos:pos + 2]
        pos += 2
    o_ref, stf_ref, stb_ref = refs[pos:pos + 3]

    c = GLA_CHUNK
    n = t // c
    if has_init:
        stf_ref[...] = s0f_ref[...].T
        stb_ref[...] = s0b_ref[...].T
    else:
        stf_ref[...] = jnp.zeros_like(stf_ref)
        stb_ref[...] = jnp.zeros_like(stb_ref)

    wup_f, wup_b = wup_ref[0], wup_ref[1]
    bup_f, bup_b = bup_ref[0], bup_ref[1]

    def rows(r):
        return pl.ds(pl.multiple_of(r * c, c), c)

    def make_step(first_touch):
        def step(i, carry):
            jobs = []
            for u in range(unroll):
                cf = i * unroll + u
                jobs.append((rows(cf), wup_f, bup_f, stf_ref, False))
                jobs.append((rows(n - 1 - cf), wup_b, bup_b, stb_ref, True))
            preps = [_gla_prep(q_ref[r, :], k_ref[r, :], v_ref[r, :], gk_ref[r, :], w, bias, rev)
                     for r, w, bias, _, rev in jobs]
            for (r, _, _, st_ref, _), prep in zip(jobs, preps):
                o = _gla_state_step(prep, v_ref[r, :], st_ref)
                if first_touch:
                    o_ref[r, :] = o
                else:
                    o_ref[r, :] += o
            return carry
        return step

    half = n // (2 * unroll)
    lax.fori_loop(0, half, make_step(True), 0)
    lax.fori_loop(half, 2 * half, make_step(False), 0)

    if emit_state:
        sf_ref[...] = stf_ref[...].T
        sb_ref[...] = stb_ref[...].T

    blk = min(t, 256)

    def fin(i, carry):
        r = pl.multiple_of(i * blk, blk)
        o = o_ref[pl.ds(r, blk), :]
        ms = jnp.mean(o * o, axis=-1, keepdims=True)
        og = og_ref[pl.ds(r, blk), :].astype(F32)
        y = o * lax.rsqrt(ms + NORM_EPS) * gn_ref[...] * (og * _sigmoid(og))
        y_ref[pl.ds(r, blk), :] = y.astype(BF16)
        return carry

    lax.fori_loop(0, t // blk, fin, 0)


def _gla(z, gk_lo, w_gk_up, b_gk, gn, n_seq, t, dk, dv, s0f=None, s0b=None, emit_state=False):
    h = GLA_HEADS
    m = n_seq * t
    n_chunks = t // GLA_CHUNK
    unroll = max(1, min(GLA_UNROLL, n_chunks // 2))
    assert t % (2 * unroll * GLA_CHUNK) == 0
    has_init = s0f is not None
    qk_blocks = (h * dk) // dk
    in_specs = [
        pl.BlockSpec((t, dk), lambda s, hh: (s, hh)),
        pl.BlockSpec((t, dk), lambda s, hh: (s, qk_blocks + hh)),
        pl.BlockSpec((t, dv), lambda s, hh: (s, (2 * h * dk) // dv + hh)),
        pl.BlockSpec((t, dv), lambda s, hh: (s, (2 * h * dk) // dv + h + hh)),
        pl.BlockSpec((t, 2 * GK_RANK), lambda s, hh: (s, 0)),
        pl.BlockSpec((2, 2 * GK_RANK, dk), lambda s, hh: (0, 0, hh)),
        pl.BlockSpec((2, 1, dk), lambda s, hh: (0, 0, hh)),
        pl.BlockSpec((1, dv), lambda s, hh: (0, 0)),
    ]
    args = [z, z, z, z, gk_lo, w_gk_up, b_gk.reshape(2, 1, h * dk), gn.reshape(1, dv)]
    if has_init:
        in_specs += [pl.BlockSpec((None, dk, dv), lambda s, hh: (s * h + hh, 0, 0))] * 2
        args += [s0f.reshape(n_seq * h, dk, dv), s0b.reshape(n_seq * h, dk, dv)]
    out_specs = [pl.BlockSpec((t, dv), lambda s, hh: (s, hh))]
    out_shape = [jax.ShapeDtypeStruct((m, h * dv), BF16)]
    if emit_state:
        out_specs += [pl.BlockSpec((None, dk, dv), lambda s, hh: (s * h + hh, 0, 0))] * 2
        out_shape += [jax.ShapeDtypeStruct((n_seq * h, dk, dv), F32)] * 2
    return pl.pallas_call(
        functools.partial(_gla_body, t=t, unroll=unroll, has_init=has_init, emit_state=emit_state),
        grid=(n_seq, h),
        in_specs=in_specs,
        out_specs=out_specs,
        out_shape=out_shape,
        scratch_shapes=[pltpu.VMEM((t, dv), F32), pltpu.VMEM((dv, dk), F32), pltpu.VMEM((dv, dk), F32)],
        compiler_params=_cparams(("arbitrary", "arbitrary")),
        name="gla",
    )(*args)


def _mix_body(yg_ref, cx_ref, cb_ref, cc_ref, ga_ref, gb_ref, cw_ref, wg_ref, wc_ref, o_ref, *, seg):
    tm = yg_ref.shape[0]
    s = cc_ref[...].astype(F32) * cx_ref[...].astype(F32)
    rowv = lax.broadcasted_iota(I32, (tm, 1), 0)
    pos = rowv & (seg - 1)
    prev = jnp.where(pos == 0, 0.0, pltpu.roll(s, 1, axis=0))
    nxt = jnp.where(pos == seg - 1, 0.0, pltpu.roll(s, tm - 1, axis=0))
    conv = prev * cw_ref[0:1, :] + s * cw_ref[1:2, :] + nxt * cw_ref[2:3, :]
    u = (cb_ref[...].astype(F32) * conv).astype(BF16)
    y_gla = _dot(yg_ref[...], wg_ref[...])
    y_conv = _dot(u, wc_ref[...])
    merged = _sigmoid(ga_ref[...].astype(F32)) * y_gla + _sigmoid(gb_ref[...].astype(F32)) * y_conv
    o_ref[...] = merged.astype(BF16)


def _mix(yg, z, conv_w, w_gla_out, w_conv_out, tm, seg, d, col0):
    m = yg.shape[0]
    zspec = lambda c: pl.BlockSpec((tm, d), lambda i: (i, col0 + c))
    const = lambda shape: pl.BlockSpec(shape, lambda i: (0, 0))
    return pl.pallas_call(
        functools.partial(_mix_body, seg=seg),
        grid=(m // tm,),
        in_specs=[pl.BlockSpec((tm, d), lambda i: (i, 0)), zspec(0), zspec(1), zspec(2), zspec(3), zspec(4),
                  const(conv_w.shape), const(w_gla_out.shape), const(w_conv_out.shape)],
        out_specs=pl.BlockSpec((tm, d), lambda i: (i, 0)),
        out_shape=jax.ShapeDtypeStruct((m, d), BF16),
        compiler_params=_cparams(("arbitrary",)),
        name="mix",
    )(yg, z, z, z, z, z, conv_w, w_gla_out, w_conv_out)


def _route_body(mg_ref, x_ref, mod_ref, wo_ref, g_ref, wr_ref, br_ref, cnt_in_ref,
                x2_ref, hp_ref, route_ref, cnt_ref, carry_ref, *, d, n_exp):
    i = pl.program_id(0)
    tm = x_ref.shape[0]

    @pl.when(i == 0)
    def _():
        carry_ref[...] = cnt_in_ref[...]

    x2 = x_ref[...] + mod_ref[:, 2 * d:3 * d] * _dot(mg_ref[...], wo_ref[...])
    x2_ref[...] = x2
    ms = jnp.mean(x2 * x2, axis=-1, keepdims=True)
    h = x2 * lax.rsqrt(ms + NORM_EPS) * g_ref[...] * (1.0 + mod_ref[:, 4 * d:5 * d]) + mod_ref[:, 3 * d:4 * d]
    hp_ref[...] = h.astype(BF16).reshape(tm, d // LANES, LANES)

    logits = _dot_f32(h, wr_ref[...]) + br_ref[...]
    lane = lax.broadcasted_iota(I32, (tm, n_exp), 1).astype(F32)
    work = logits
    vals, hots = [], []
    for _ in range(TOP_K):
        mx = jnp.max(work, axis=-1, keepdims=True)
        idx = jnp.min(jnp.where(work == mx, lane, float(n_exp)), axis=-1, keepdims=True)
        hot = lane == idx
        vals.append(mx)
        hots.append(hot)
        work = jnp.where(hot, -jnp.inf, work)
    exps = [jnp.exp(v - vals[0]) for v in vals]
    inv = 1.0 / (exps[0] + exps[1] + exps[2] + exps[3])

    cnt = jnp.zeros((tm, n_exp), F32)
    for hot in hots:
        cnt = cnt + jnp.where(hot, 1.0, 0.0)
    r = lax.broadcasted_iota(I32, (tm, tm), 0)
    cc = lax.broadcasted_iota(I32, (tm, tm), 1)
    strict = jnp.where(cc < r, 1.0, 0.0).astype(BF16)
    before = _dot(strict, cnt.astype(BF16)) + carry_ref[...]
    carry_ref[...] = carry_ref[...] + jnp.sum(cnt, axis=0, keepdims=True)

    out_lane = lax.broadcasted_iota(I32, (tm, LANES), 1)
    route = jnp.zeros((tm, LANES), I32)
    for k in range(TOP_K):
        e_k = jnp.sum(jnp.where(hots[k], lane, 0.0), axis=-1, keepdims=True).astype(I32)
        rank_k = jnp.sum(jnp.where(hots[k], before, 0.0), axis=-1, keepdims=True).astype(I32)
        p_k = lax.bitcast_convert_type(exps[k] * inv, I32)
        route = jnp.where(out_lane == k, e_k, route)
        route = jnp.where(out_lane == TOP_K + k, rank_k, route)
        route = jnp.where(out_lane == 2 * TOP_K + k, p_k, route)
    route_ref[...] = route

    @pl.when(i == pl.num_programs(0) - 1)
    def _():
        cnt_ref[...] = carry_ref[...]


def _route(merged, x, mod3, mod_base, tiles_per_mod, w_o, norm_g, w_router, b_router, counts_in, tm):
    m, d = x.shape
    n_exp = w_router.shape[1]
    const = lambda shape: pl.BlockSpec(shape, lambda i: (0, 0))
    tile = lambda w: pl.BlockSpec((tm, w), lambda i: (i, 0))
    return pl.pallas_call(
        functools.partial(_route_body, d=d, n_exp=n_exp),
        grid=(m // tm,),
        in_specs=[tile(d), tile(d),
                  pl.BlockSpec((None, 1, mod3.shape[2]), lambda i: (mod_base + i // tiles_per_mod, 0, 0)),
                  const(w_o.shape), const((1, d)), const(w_router.shape), const((1, n_exp)), const((1, n_exp))],
        out_specs=[tile(d), pl.BlockSpec((tm, d // LANES, LANES), lambda i: (i, 0, 0)), tile(LANES),
                   const((1, n_exp))],
        out_shape=[jax.ShapeDtypeStruct((m, d), F32), jax.ShapeDtypeStruct((m, d // LANES, LANES), BF16),
                   jax.ShapeDtypeStruct((m, LANES), I32), jax.ShapeDtypeStruct((1, n_exp), F32)],
        scratch_shapes=[pltpu.VMEM((1, n_exp), F32)],
        compiler_params=_cparams(("arbitrary",)),
        name="route",
    )(merged, x, mod3, w_o, norm_g.reshape(1, d), w_router, b_router.reshape(1, n_exp), counts_in)


def _dispatch_body(pos_ref, hp_ref, xs_in_ref, xs_ref, sem, *, tb):
    del xs_in_ref
    i = pl.program_id(0)
    n = tb * TOP_K

    def issue(r, carry):
        for k in range(TOP_K):
            slot = pos_ref[i * n + r * TOP_K + k]
            pltpu.make_async_copy(hp_ref.at[r], xs_ref.at[slot], sem).start()
        return carry

    lax.fori_loop(0, tb, issue, 0, unroll=4)
    for k in range(TOP_K):
        pltpu.make_async_copy(hp_ref, xs_ref.at[pl.ds(0, tb)], sem).wait()


def _dispatch(pos_flat, hp, xs, tb):
    m, nt, _ = hp.shape
    return pl.pallas_call(
        functools.partial(_dispatch_body, tb=tb),
        grid_spec=pltpu.PrefetchScalarGridSpec(
            num_scalar_prefetch=1,
            grid=(m // tb,),
            in_specs=[pl.BlockSpec((tb, nt, LANES), lambda i, pos: (i, 0, 0)), pl.BlockSpec(memory_space=pl.ANY)],
            out_specs=pl.BlockSpec(memory_space=pl.ANY),
            scratch_shapes=[pltpu.SemaphoreType.DMA(())],
        ),
        out_shape=jax.ShapeDtypeStruct(xs.shape, xs.dtype),
        input_output_aliases={2: 0},
        compiler_params=_cparams(("arbitrary",)),
        name="dispatch",
    )(pos_flat, hp, xs)


def _experts_body(e_ref, xb_ref, valid_ref, x_ref, w1g_ref, w1u_ref, b1g_ref, b1u_ref, w2_ref, b2_ref,
                  y_ref, xbf_ref, act_ref, out_ref, *, n_fa, n_jb):
    del e_ref, xb_ref
    i = pl.program_id(0)
    s = pl.program_id(1)
    valid = valid_ref[i]
    rb, d = xbf_ref.shape
    nt = d // LANES
    n_sub = rb // SUB_TILE
    nv = (valid + SUB_TILE - 1) // SUB_TILE

    @pl.when(s == 0)
    def _():
        for t in range(n_sub):
            rows = slice(t * SUB_TILE, (t + 1) * SUB_TILE)

            @pl.when(t < nv)
            def _():
                xbf_ref[rows, :] = x_ref[rows].reshape(SUB_TILE, d)

            @pl.when(t >= nv)
            def _():
                y_ref[rows] = jnp.zeros((SUB_TILE, nt, LANES), BF16)

    for v in range(1, n_sub + 1):
        m = v * SUB_TILE

        @pl.when((nv == v) & (s < n_fa))
        def _():
            xs = xbf_ref[0:m, :]
            gate = jnp.minimum(_dot(xs, w1g_ref[...].astype(BF16)) + b1g_ref[...], SWIGLU_LIMIT)
            up = jnp.clip(_dot(xs, w1u_ref[...].astype(BF16)) + b1u_ref[...], -SWIGLU_LIMIT, SWIGLU_LIMIT)
            act = (up + 1.0) * gate * _sigmoid(SWIGLU_ALPHA * gate)
            act_ref[s, 0:m, :] = act.astype(BF16)

        @pl.when((nv == v) & (s >= n_fa))
        def _():
            a = jnp.concatenate([act_ref[f, 0:m, :] for f in range(n_fa)], axis=1)
            out = (_dot(a, w2_ref[...].astype(BF16)) + b2_ref[...]).astype(BF16)
            out_ref[s - n_fa, 0:m, :] = out

        @pl.when((nv == v) & (s == n_fa + n_jb - 1))
        def _():
            full = jnp.concatenate([out_ref[j, 0:m, :] for j in range(n_jb)], axis=1)
            y_ref[0:m] = full.reshape(m, nt, LANES)


def _experts(item_e, item_xb, item_valid, xs, w1, b1, w2, b2):
    n_items = item_e.shape[0]
    n_exp, d, ff2 = w1.shape
    ff = ff2 // 2
    nt = d // LANES
    rb = ROW_BLOCK
    n_fa = ff // FF_CHUNK
    n_jb = d // OUT_CHUNK

    def fa(i, s, v):
        return jnp.where(v[i] > 0, jnp.minimum(s, n_fa - 1), n_fa - 1)

    def jb(i, s, v):
        return jnp.where(v[i] > 0, jnp.maximum(s - n_fa, 0), n_jb - 1)

    return pl.pallas_call(
        functools.partial(_experts_body, n_fa=n_fa, n_jb=n_jb),
        grid_spec=pltpu.PrefetchScalarGridSpec(
            num_scalar_prefetch=3,
            grid=(n_items, n_fa + n_jb),
            in_specs=[
                pl.BlockSpec((rb, nt, LANES), lambda i, s, e, xb, v: (xb[i], 0, 0)),
                pl.BlockSpec((None, d, FF_CHUNK), lambda i, s, e, xb, v: (e[i], 0, fa(i, s, v))),
                pl.BlockSpec((None, d, FF_CHUNK), lambda i, s, e, xb, v: (e[i], 0, n_fa + fa(i, s, v))),
                pl.BlockSpec((None, 1, FF_CHUNK), lambda i, s, e, xb, v: (e[i], 0, fa(i, s, v))),
                pl.BlockSpec((None, 1, FF_CHUNK), lambda i, s, e, xb, v: (e[i], 0, n_fa + fa(i, s, v))),
                pl.BlockSpec((None, ff, OUT_CHUNK), lambda i, s, e, xb, v: (e[i], 0, jb(i, s, v))),
                pl.BlockSpec((None, 1, OUT_CHUNK), lambda i, s, e, xb, v: (e[i], 0, jb(i, s, v))),
            ],
            out_specs=pl.BlockSpec((rb, nt, LANES), lambda i, s, e, xb, v: (i, 0, 0)),
            scratch_shapes=[pltpu.VMEM((rb, d), BF16), pltpu.VMEM((n_fa, rb, FF_CHUNK), BF16),
                            pltpu.VMEM((n_jb, rb, OUT_CHUNK), BF16)],
        ),
        out_shape=jax.ShapeDtypeStruct((n_items * rb, nt, LANES), BF16),
        compiler_params=_cparams(("arbitrary", "arbitrary")),
        name="experts",
    )(item_e, item_xb, item_valid, xs, w1, w1, b1.reshape(n_exp, 1, ff2), b1.reshape(n_exp, 1, ff2),
      w2, b2.reshape(n_exp, 1, d))


def _combine_body(pos_ref, route_ref, x2_ref, mod_ref, g_ref, y_hbm, o_ref, buf_ref, sem, *, tb, d):
    i = pl.program_id(0)
    n_steps = pl.num_programs(0)
    n = tb * TOP_K

    def issue_block(blk, slot):
        def issue(r, carry):
            for k in range(TOP_K):
                src = y_hbm.at[pos_ref[blk * n + r * TOP_K + k]]
                pltpu.make_async_copy(src, buf_ref.at[slot, k, r], sem.at[slot]).start()
            return carry

        lax.fori_loop(0, tb, issue, 0, unroll=4)

    @pl.when(i == 0)
    def _():
        issue_block(0, 0)

    @pl.when(i + 1 < n_steps)
    def _():
        issue_block(i + 1, (i + 1) % 2)

    slot = i % 2
    for k in range(TOP_K):
        pltpu.make_async_copy(y_hbm.at[pl.ds(0, tb)], buf_ref.at[slot, k], sem.at[slot]).wait()

    route = route_ref[...]
    moe = jnp.zeros((tb, d), F32)
    for k in range(TOP_K):
        p_k = lax.bitcast_convert_type(route[:, 2 * TOP_K + k:2 * TOP_K + k + 1], F32)
        moe = moe + p_k * buf_ref[slot, k].reshape(tb, d).astype(F32)
    x = x2_ref[...] + mod_ref[:, 5 * d:6 * d] * moe
    ms = jnp.mean(x * x, axis=-1, keepdims=True)
    o_ref[...] = x * lax.rsqrt(ms + NORM_EPS) * g_ref[...]


def _combine(pos_flat, route, x2, mod3, mod_base, tiles_per_mod, final_g, y_sorted, tb):
    m, d = x2.shape
    return pl.pallas_call(
        functools.partial(_combine_body, tb=tb, d=d),
        grid_spec=pltpu.PrefetchScalarGridSpec(
            num_scalar_prefetch=1,
            grid=(m // tb,),
            in_specs=[
                pl.BlockSpec((tb, LANES), lambda i, pos: (i, 0)),
                pl.BlockSpec((tb, d), lambda i, pos: (i, 0)),
                pl.BlockSpec((None, 1, mod3.shape[2]), lambda i, pos: (mod_base + i // tiles_per_mod, 0, 0)),
                pl.BlockSpec((1, d), lambda i, pos: (0, 0)),
                pl.BlockSpec(memory_space=pl.ANY),
            ],
            out_specs=pl.BlockSpec((tb, d), lambda i, pos: (i, 0)),
            scratch_shapes=[pltpu.VMEM((2, TOP_K, tb, d // LANES, LANES), BF16), pltpu.SemaphoreType.DMA((2,))],
        ),
        out_shape=jax.ShapeDtypeStruct((m, d), F32),
        compiler_params=_cparams(("arbitrary",)),
        name="combine",
    )(pos_flat, route, x2, mod3, final_g.reshape(1, d), y_sorted)


def _is_pow2(v):
    return v > 0 and (v & (v - 1)) == 0


def kernel(x_prompt, x_sample, state_gla_fwd, state_gla_bwd, c, c_ctx, norm1_g, w_ada, b_ada, w_in, w_gk_up, b_gk,
           gla_norm_g, w_gla_out, conv_w, w_conv_out, w_o, norm2_g, w_router, b_router, w_exp1, b_exp1, w_exp2,
           b_exp2, final_norm_g):
    batch, seq, d = x_prompt.shape
    dec_batch, dec_seq, _ = x_sample.shape
    depth = norm1_g.shape[0]
    assert depth == 1, "single trunk layer"
    h = GLA_HEADS
    dk_all = w_gk_up.shape[-1]
    dk, dv = dk_all // h, d // h
    n_exp = w_router.shape[-1]
    grid_w = 64
    assert _is_pow2(seq) and _is_pow2(grid_w) and dec_seq % grid_w == 0
    tm = 256
    assert tm % seq == 0 and tm % grid_w == 0, "conv segments must not straddle token tiles"

    c_gk = 2 * dk_all + 2 * d
    w_in0 = w_in[0]
    w_main = jnp.concatenate([w_in0[:, :c_gk], w_in0[:, c_gk + 2 * GK_RANK:]], axis=1).astype(BF16)
    w_gk = w_in0[:, c_gk:c_gk + 2 * GK_RANK]
    zr = jnp.zeros((GK_RANK, dk_all), F32)
    w_up2 = jnp.stack([jnp.concatenate([w_gk_up[0, 0], zr], axis=0), jnp.concatenate([zr, w_gk_up[0, 1]], axis=0)])
    w_gla_out_b = w_gla_out[0].astype(BF16)
    w_conv_out_b = w_conv_out[0].astype(BF16)
    w_o_b = w_o[0].astype(BF16)

    n_cond = 1 + dec_batch
    cond = jnp.concatenate([c_ctx[None, :], c, jnp.zeros((8 - n_cond % 8, d), F32)], axis=0)
    mod = _ada(cond, w_ada[0], b_ada[0])
    mod3 = mod.reshape(mod.shape[0], 1, mod.shape[1])

    conv_col0 = (2 * dk_all + 2 * d) // d

    passes = []
    counts = jnp.zeros((1, n_exp), F32)
    for name, x3, n_seq, t, mod_base, seg in (("ctx", x_prompt, batch, seq, 0, seq),
                                              ("lat", x_sample, dec_batch, dec_seq, 1, grid_w)):
        m = n_seq * t
        x = x3.reshape(m, d)
        tiles_per_mod = (m // tm) if name == "ctx" else (t // tm)
        hb, gk_lo = _pre(x, mod3, mod_base, tiles_per_mod, norm1_g[0], w_gk, tm)
        z = _matmul(hb, w_main, min(m, 2048), 1024)
        if name == "ctx":
            yg, s_f, s_b = _gla(z, gk_lo, w_up2, b_gk[0], gla_norm_g[0], n_seq, t, dk, dv, emit_state=True)
        else:
            (yg,) = _gla(z, gk_lo, w_up2, b_gk[0], gla_norm_g[0], n_seq, t, dk, dv,
                         s0f=state_gla_fwd[:, 0], s0b=state_gla_bwd[:, 0])
            s_f = s_b = None
        merged = _mix(yg, z, conv_w[0], w_gla_out_b, w_conv_out_b, tm, seg, d, conv_col0)
        x2, hp, route, counts = _route(merged, x, mod3, mod_base, tiles_per_mod, w_o_b, norm2_g[0],
                                       w_router[0], b_router[0], counts, tm)
        passes.append(dict(x2=x2, hp=hp, route=route, s_f=s_f, s_b=s_b, mod_base=mod_base,
                           tiles_per_mod=tiles_per_mod, shape=x3.shape))

    n_tok = sum(p["x2"].shape[0] for p in passes)
    n_items = (n_tok * TOP_K) // ROW_BLOCK + n_exp
    cnt = counts[0].astype(I32)
    nblk = (cnt + ROW_BLOCK - 1) // ROW_BLOCK
    blk_end = jnp.cumsum(nblk)
    blk0 = blk_end - nblk
    total = blk_end[-1]
    item = jnp.arange(n_items, dtype=I32)
    owner = lambda blk: jnp.sum((blk_end[None, :] <= blk[:, None]).astype(I32), axis=1)
    live = item < total
    item_e = jnp.where(live, jnp.minimum(owner(item), n_exp - 1), owner(total[None] - 1)[0])
    item_xb = jnp.where(live, item, total - 1)
    item_valid = jnp.where(live, jnp.clip(cnt[item_e] - (item - blk0[item_e]) * ROW_BLOCK, 0, ROW_BLOCK), 0)
    slot0 = blk0 * ROW_BLOCK

    xs = jnp.zeros((n_items * ROW_BLOCK, d // LANES, LANES), BF16)
    for p in passes:
        e_idx = p["route"][:, 0:TOP_K]
        rank = p["route"][:, TOP_K:2 * TOP_K]
        p["pos"] = (slot0[e_idx] + rank).reshape(-1)
        xs = _dispatch(p["pos"], p["hp"], xs, tm)

    y_sorted = _experts(item_e, item_xb, item_valid, xs, w_exp1[0], b_exp1[0], w_exp2[0], b_exp2[0])

    outs = []
    for p in passes:
        y = _combine(p["pos"], p["route"], p["x2"], mod3, p["mod_base"], p["tiles_per_mod"], final_norm_g,
                     y_sorted, tm)
        outs.append(y.reshape(p["shape"]))

    st_shape = (batch, depth, h, dk, dv)
    return (outs[0], outs[1], passes[0]["s_f"].reshape(st_shape), passes[0]["s_b"].reshape(st_shape))
```

```python
import functools

import jax
import jax.numpy as jnp
from jax import lax
from jax.experimental import pallas as pl
from jax.experimental.pallas import tpu as pltpu

F32 = jnp.float32
BF16 = jnp.bfloat16
I32 = jnp.int32

GLA_HEADS = 4
GK_RANK = 16
GATE_NORMALIZER = 16.0
TOP_K = 4
SWIGLU_ALPHA = 1.702
SWIGLU_LIMIT = 7.0
NORM_EPS = 1e-6

LANES = 128
BF16_ROWS = 16
VMEM_LIMIT = 56 * 1024 * 1024

GLA_CHUNK = 64
GLA_SUB = BF16_ROWS
GLA_UNROLL = 4
NEG_BIG = -1e30

ROW_BLOCK = 1024
SUB_TILE = 256
FF_CHUNK = 512
OUT_CHUNK = 512


def _cparams(sem):
    return pltpu.CompilerParams(dimension_semantics=sem, vmem_limit_bytes=VMEM_LIMIT)


def _dot(a, b):
    return jnp.dot(a, b, preferred_element_type=F32)


def _dot_nt(a, b):
    return lax.dot_general(a, b, (((1,), (1,)), ((), ())), preferred_element_type=F32)


def _dot_tn(a, b):
    return lax.dot_general(a, b, (((0,), (0,)), ((), ())), preferred_element_type=F32)


def _split_bf16(a, n):
    parts = []
    r = a
    for _ in range(n):
        p = r.astype(BF16)
        parts.append(p)
        r = r - p.astype(F32)
    return parts


def _dot_f32(a, b):
    a_hi, a_lo = _split_bf16(a, 2)
    b_hi, b_lo = _split_bf16(b, 2)
    return _dot(a_hi, b_hi) + (_dot(a_lo, b_hi) + _dot(a_hi, b_lo))


def _sigmoid(x):
    return 1.0 / (1.0 + jnp.exp(-x))


def _resident(shape):
    return pl.BlockSpec(shape, lambda i: (0,) * len(shape), pipeline_mode=pl.Buffered(1))


def _ada_body(c_ref, w_ref, b_ref, o_ref):
    c = c_ref[...]
    s = c * _sigmoid(c)
    o_ref[...] = _dot(s.astype(BF16), w_ref[...].astype(BF16)) + b_ref[...]


def _ada(cond, w_ada, b_ada):
    rows, d = cond.shape
    n = w_ada.shape[1]
    bn = 1024
    return pl.pallas_call(
        _ada_body,
        grid=(n // bn,),
        in_specs=[
            pl.BlockSpec((rows, d), lambda j: (0, 0)),
            pl.BlockSpec((d, bn), lambda j: (0, j)),
            pl.BlockSpec((1, bn), lambda j: (0, j)),
        ],
        out_specs=pl.BlockSpec((rows, bn), lambda j: (0, j)),
        out_shape=jax.ShapeDtypeStruct((rows, n), F32),
        compiler_params=_cparams(("arbitrary",)),
        name="ada",
    )(cond, w_ada, b_ada.reshape(1, n))


def _pre_body(x_ref, mod_ref, g_ref, wgk_ref, h_ref, gk_ref, *, d):
    x = x_ref[...]
    ms = jnp.mean(x * x, axis=-1, keepdims=True)
    y = x * lax.rsqrt(ms + NORM_EPS) * g_ref[...]
    h = y * (1.0 + mod_ref[:, d:2 * d]) + mod_ref[:, 0:d]
    h_ref[...] = h.astype(BF16)
    gk_ref[...] = _dot_f32(h, wgk_ref[...])


def _pre(x, mod3, mod_base, tiles_per_mod, norm_g, w_gk, tm):
    m, d = x.shape
    r = w_gk.shape[1]
    return pl.pallas_call(
        functools.partial(_pre_body, d=d),
        grid=(m // tm,),
        in_specs=[
            pl.BlockSpec((tm, d), lambda i: (i, 0)),
            pl.BlockSpec((None, 1, mod3.shape[2]), lambda i: (mod_base + i // tiles_per_mod, 0, 0)),
            pl.BlockSpec((1, d), lambda i: (0, 0)),
            pl.BlockSpec((d, r), lambda i: (0, 0)),
        ],
        out_specs=[
            pl.BlockSpec((tm, d), lambda i: (i, 0)),
            pl.BlockSpec((tm, r), lambda i: (i, 0)),
        ],
        out_shape=[jax.ShapeDtypeStruct((m, d), BF16), jax.ShapeDtypeStruct((m, r), F32)],
        compiler_params=_cparams(("arbitrary",)),
        name="pre",
    )(x, mod3, norm_g.reshape(1, d), w_gk)


def _mm_body(a_ref, b_ref, o_ref):
    o_ref[...] = _dot(a_ref[...], b_ref[...]).astype(o_ref.dtype)


def _matmul(a, b, bm, bn):
    m, k = a.shape
    n = b.shape[1]
    while n % bn:
        bn //= 2
    assert bn % LANES == 0
    return pl.pallas_call(
        _mm_body,
        grid=(m // bm, n // bn),
        in_specs=[
            pl.BlockSpec((bm, k), lambda i, j: (i, 0)),
            pl.BlockSpec((k, bn), lambda i, j: (0, j)),
        ],
        out_specs=pl.BlockSpec((bm, bn), lambda i, j: (i, j)),
        out_shape=jax.ShapeDtypeStruct((m, n), BF16),
        compiler_params=_cparams(("arbitrary", "arbitrary")),
        name="in_proj",
    )(a, b)


def _gla_prep(q, k, v, gk_lo, w_up, b_up, reverse):
    c, dk = q.shape
    gk = _dot_f32(gk_lo, w_up) + b_up
    g = (jnp.minimum(gk, 0.0) - jnp.log(1.0 + jnp.exp(-jnp.abs(gk)))) * (1.0 / GATE_NORMALIZER)
    row = lax.broadcasted_iota(I32, (c, c), 0)
    col = lax.broadcasted_iota(I32, (c, c), 1)
    tri = (col >= row) if reverse else (col <= row)
    tri_b = jnp.where(tri, 1.0, 0.0).astype(BF16)
    g1, g2 = _split_bf16(g, 2)
    b = _dot(tri_b, g1) + _dot(tri_b, g2)

    qf = q.astype(F32) * (dk ** -0.5)
    kf = k.astype(F32)
    q_in = (qf * jnp.exp(b)).astype(BF16)

    rowv = lax.broadcasted_iota(I32, (c, 1), 0)
    lhs, rhs = [], []
    for j in range(c // GLA_SUB):
        lo, hi = j * GLA_SUB, (j + 1) * GLA_SUB
        e_row = lo if reverse else hi - 1
        e = b[e_row:e_row + 1, :]
        in_blk = (rowv >= lo) & (rowv < hi)
        active = (rowv < hi) if reverse else (rowv >= lo)
        lhs.append((qf * jnp.exp(jnp.where(active, b - e, NEG_BIG))).astype(BF16))
        rhs.append((kf * jnp.exp(jnp.where(in_blk, e - b, NEG_BIG))).astype(BF16))
    p = _dot_nt(jnp.concatenate(lhs, axis=1), jnp.concatenate(rhs, axis=1))
    p = jnp.where(tri, p, 0.0)
    o_intra = _dot(p.astype(BF16), v)

    l_row = 0 if reverse else c - 1
    bl = b[l_row:l_row + 1, :]
    ks = (kf * jnp.exp(bl - b)).astype(BF16)
    return q_in, ks, jnp.exp(bl), o_intra


def _gla_state_step(prep, v, st_ref):
    q_in, ks, decay, o_intra = prep
    st = st_ref[...]
    o = o_intra + _dot_nt(q_in, st.astype(BF16))
    st_ref[...] = st * decay + _dot_tn(v, ks)
    return o


def _gla_body(*refs, t, unroll, has_init, emit_state):
    q_ref, k_ref, v_ref, og_ref, gk_ref, wup_ref, bup_ref, gn_ref = refs[:8]
    pos = 8
    if has_init:
        s0f_ref, s0b_ref = refs[pos:pos + 2]
        pos += 2
    y_ref = refs[pos]
    pos += 1
    if emit_state:
        sf_ref, sb_ref = refs[pos:pos + 2]
        pos += 2
    o_ref, stf_ref, stb_ref = refs[pos:pos + 3]

    c = GLA_CHUNK
    n = t // c
    if has_init:
        stf_ref[...] = s0f_ref[...].T
        stb_ref[...] = s0b_ref[...].T
    else:
        stf_ref[...] = jnp.zeros_like(stf_ref)
        stb_ref[...] = jnp.zeros_like(stb_ref)

    wup_f, wup_b = wup_ref[0], wup_ref[1]
    bup_f, bup_b = bup_ref[0], bup_ref[1]

    def rows(r):
        return pl.ds(pl.multiple_of(r * c, c), c)

    def make_step(first_touch):
        def step(i, carry):
            jobs = []
            for u in range(unroll):
                cf = i * unroll + u
                jobs.append((rows(cf), wup_f, bup_f, stf_ref, False))
                jobs.append((rows(n - 1 - cf), wup_b, bup_b, stb_ref, True))
            preps = [_gla_prep(q_ref[r, :], k_ref[r, :], v_ref[r, :], gk_ref[r, :], w, bias, rev)
                     for r, w, bias, _, rev in jobs]
            for (r, _, _, st_ref, _), prep in zip(jobs, preps):
                o = _gla_state_step(prep, v_ref[r, :], st_ref)
                if first_touch:
                    o_ref[r, :] = o
                else:
                    o_ref[r, :] += o
            return carry
        return step

    half = n // (2 * unroll)
    lax.fori_loop(0, half, make_step(True), 0)
    lax.fori_loop(half, 2 * half, make_step(False), 0)

    if emit_state:
        sf_ref[...] = stf_ref[...].T
        sb_ref[...] = stb_ref[...].T

    blk = min(t, 256)

    def fin(i, carry):
        r = pl.multiple_of(i * blk, blk)
        o = o_ref[pl.ds(r, blk), :]
        ms = jnp.mean(o * o, axis=-1, keepdims=True)
        og = og_ref[pl.ds(r, blk), :].astype(F32)
        y = o * lax.rsqrt(ms + NORM_EPS) * gn_ref[...] * (og * _sigmoid(og))
        y_ref[pl.ds(r, blk), :] = y.astype(BF16)
        return carry

    lax.fori_loop(0, t // blk, fin, 0)


def _gla(z, gk_lo, w_gk_up, b_gk, gn, n_seq, t, dk, dv, s0f=None, s0b=None, emit_state=False):
    h = GLA_HEADS
    m = n_seq * t
    n_chunks = t // GLA_CHUNK
    unroll = max(1, min(GLA_UNROLL, n_chunks // 2))
    assert t % (2 * unroll * GLA_CHUNK) == 0
    has_init = s0f is not None
    qk_blocks = (h * dk) // dk
    in_specs = [
        pl.BlockSpec((t, dk), lambda s, hh: (s, hh)),
        pl.BlockSpec((t, dk), lambda s, hh: (s, qk_blocks + hh)),
        pl.BlockSpec((t, dv), lambda s, hh: (s, (2 * h * dk) // dv + hh)),
        pl.BlockSpec((t, dv), lambda s, hh: (s, (2 * h * dk) // dv + h + hh)),
        pl.BlockSpec((t, 2 * GK_RANK), lambda s, hh: (s, 0)),
        pl.BlockSpec((2, 2 * GK_RANK, dk), lambda s, hh: (0, 0, hh)),
        pl.BlockSpec((2, 1, dk), lambda s, hh: (0, 0, hh)),
        pl.BlockSpec((1, dv), lambda s, hh: (0, 0)),
    ]
    args = [z, z, z, z, gk_lo, w_gk_up, b_gk.reshape(2, 1, h * dk), gn.reshape(1, dv)]
    if has_init:
        in_specs += [pl.BlockSpec((None, dk, dv), lambda s, hh: (s * h + hh, 0, 0))] * 2
        args += [s0f.reshape(n_seq * h, dk, dv), s0b.reshape(n_seq * h, dk, dv)]
    out_specs = [pl.BlockSpec((t, dv), lambda s, hh: (s, hh))]
    out_shape = [jax.ShapeDtypeStruct((m, h * dv), BF16)]
    if emit_state:
        out_specs += [pl.BlockSpec((None, dk, dv), lambda s, hh: (s * h + hh, 0, 0))] * 2
        out_shape += [jax.ShapeDtypeStruct((n_seq * h, dk, dv), F32)] * 2
    return pl.pallas_call(
        functools.partial(_gla_body, t=t, unroll=unroll, has_init=has_init, emit_state=emit_state),
        grid=(n_seq, h),
        in_specs=in_specs,
        out_specs=out_specs,
        out_shape=out_shape,
        scratch_shapes=[pltpu.VMEM((t, dv), F32), pltpu.VMEM((dv, dk), F32), pltpu.VMEM((dv, dk), F32)],
        compiler_params=_cparams(("arbitrary", "arbitrary")),
        name="gla",
    )(*args)


def _mix_body(yg_ref, cx_ref, cb_ref, cc_ref, ga_ref, gb_ref, cw_ref, wg_ref, wc_ref, o_ref, *, seg):
    tm = yg_ref.shape[0]
    s = cc_ref[...].astype(F32) * cx_ref[...].astype(F32)
    rowv = lax.broadcasted_iota(I32, (tm, 1), 0)
    pos = rowv & (seg - 1)
    prev = jnp.where(pos == 0, 0.0, pltpu.roll(s, 1, axis=0))
    nxt = jnp.where(pos == seg - 1, 0.0, pltpu.roll(s, tm - 1, axis=0))
    conv = prev * cw_ref[0:1, :] + s * cw_ref[1:2, :] + nxt * cw_ref[2:3, :]
    u = (cb_ref[...].astype(F32) * conv).astype(BF16)
    y_gla = _dot(yg_ref[...], wg_ref[...])
    y_conv = _dot(u, wc_ref[...])
    merged = _sigmoid(ga_ref[...].astype(F32)) * y_gla + _sigmoid(gb_ref[...].astype(F32)) * y_conv
    o_ref[...] = merged.astype(BF16)


def _mix(yg, z, conv_w, w_gla_out, w_conv_out, tm, seg, d, col0):
    m = yg.shape[0]
    zspec = lambda c: pl.BlockSpec((tm, d), lambda i: (i, col0 + c))
    const = lambda shape: pl.BlockSpec(shape, lambda i: (0, 0))
    return pl.pallas_call(
        functools.partial(_mix_body, seg=seg),
        grid=(m // tm,),
        in_specs=[pl.BlockSpec((tm, d), lambda i: (i, 0)), zspec(0), zspec(1), zspec(2), zspec(3), zspec(4),
                  const(conv_w.shape), _resident(w_gla_out.shape), _resident(w_conv_out.shape)],
        out_specs=pl.BlockSpec((tm, d), lambda i: (i, 0)),
        out_shape=jax.ShapeDtypeStruct((m, d), BF16),
        compiler_params=_cparams(("arbitrary",)),
        name="mix",
    )(yg, z, z, z, z, z, conv_w, w_gla_out, w_conv_out)


def _route_body(mg_ref, x_ref, mod_ref, wo_ref, g_ref, wr_ref, br_ref, cnt_in_ref,
                x2_ref, hp_ref, route_ref, cnt_ref, carry_ref, *, d, n_exp):
    i = pl.program_id(0)
    tm = x_ref.shape[0]

    @pl.when(i == 0)
    def _():
        carry_ref[...] = cnt_in_ref[...]

    x2 = x_ref[...] + mod_ref[:, 2 * d:3 * d] * _dot(mg_ref[...], wo_ref[...])
    x2_ref[...] = x2
    ms = jnp.mean(x2 * x2, axis=-1, keepdims=True)
    h = x2 * lax.rsqrt(ms + NORM_EPS) * g_ref[...] * (1.0 + mod_ref[:, 4 * d:5 * d]) + mod_ref[:, 3 * d:4 * d]
    hp_ref[...] = h.astype(BF16).reshape(tm, d // LANES, LANES)

    logits = _dot_f32(h, wr_ref[...]) + br_ref[...]
    lane = lax.broadcasted_iota(I32, (tm, n_exp), 1).astype(F32)
    work = logits
    vals, hots = [], []
    for _ in range(TOP_K):
        mx = jnp.max(work, axis=-1, keepdims=True)
        idx = jnp.min(jnp.where(work == mx, lane, float(n_exp)), axis=-1, keepdims=True)
        hot = lane == idx
        vals.append(mx)
        hots.append(hot)
        work = jnp.where(hot, -jnp.inf, work)
    exps = [jnp.exp(v - vals[0]) for v in vals]
    inv = 1.0 / (exps[0] + exps[1] + exps[2] + exps[3])

    cnt = jnp.zeros((tm, n_exp), F32)
    for hot in hots:
        cnt = cnt + jnp.where(hot, 1.0, 0.0)
    r = lax.broadcasted_iota(I32, (tm, tm), 0)
    cc = lax.broadcasted_iota(I32, (tm, tm), 1)
    strict = jnp.where(cc < r, 1.0, 0.0).astype(BF16)
    before = _dot(strict, cnt.astype(BF16)) + carry_ref[...]
    carry_ref[...] = carry_ref[...] + jnp.sum(cnt, axis=0, keepdims=True)

    out_lane = lax.broadcasted_iota(I32, (tm, LANES), 1)
    route = jnp.zeros((tm, LANES), I32)
    for k in range(TOP_K):
        e_k = jnp.sum(jnp.where(hots[k], lane, 0.0), axis=-1, keepdims=True).astype(I32)
        rank_k = jnp.sum(jnp.where(hots[k], before, 0.0), axis=-1, keepdims=True).astype(I32)
        p_k = lax.bitcast_convert_type(exps[k] * inv, I32)
        route = jnp.where(out_lane == k, e_k, route)
        route = jnp.where(out_lane == TOP_K + k, rank_k, route)
        route = jnp.where(out_lane == 2 * TOP_K + k, p_k, route)
    route_ref[...] = route

    @pl.when(i == pl.num_programs(0) - 1)
    def _():
        cnt_ref[...] = carry_ref[...]


def _route(merged, x, mod3, mod_base, tiles_per_mod, w_o, norm_g, w_router, b_router, counts_in, tm):
    m, d = x.shape
    n_exp = w_router.shape[1]
    const = lambda shape: pl.BlockSpec(shape, lambda i: (0, 0))
    tile = lambda w: pl.BlockSpec((tm, w), lambda i: (i, 0))
    return pl.pallas_call(
        functools.partial(_route_body, d=d, n_exp=n_exp),
        grid=(m // tm,),
        in_specs=[tile(d), tile(d),
                  pl.BlockSpec((None, 1, mod3.shape[2]), lambda i: (mod_base + i // tiles_per_mod, 0, 0)),
                  _resident(w_o.shape), const((1, d)), const(w_router.shape), const((1, n_exp)),
                  const((1, n_exp))],
        out_specs=[tile(d), pl.BlockSpec((tm, d // LANES, LANES), lambda i: (i, 0, 0)), tile(LANES),
                   const((1, n_exp))],
        out_shape=[jax.ShapeDtypeStruct((m, d), F32), jax.ShapeDtypeStruct((m, d // LANES, LANES), BF16),
                   jax.ShapeDtypeStruct((m, LANES), I32), jax.ShapeDtypeStruct((1, n_exp), F32)],
        scratch_shapes=[pltpu.VMEM((1, n_exp), F32)],
        compiler_params=_cparams(("arbitrary",)),
        name="route",
    )(merged, x, mod3, w_o, norm_g.reshape(1, d), w_router, b_router.reshape(1, n_exp), counts_in)


def _dispatch_body(pos_ref, hp_ref, xs_in_ref, xs_ref, sem, *, tb):
    del xs_in_ref
    i = pl.program_id(0)
    n = tb * TOP_K

    def issue(r, carry):
        for k in range(TOP_K):
            slot = pos_ref[i * n + r * TOP_K + k]
            pltpu.make_async_copy(hp_ref.at[r], xs_ref.at[slot], sem).start()
        return carry

    lax.fori_loop(0, tb, issue, 0, unroll=4)
    for k in range(TOP_K):
        pltpu.make_async_copy(hp_ref, xs_ref.at[pl.ds(0, tb)], sem).wait()


def _dispatch(pos_flat, hp, xs, tb):
    m, nt, _ = hp.shape
    return pl.pallas_call(
        functools.partial(_dispatch_body, tb=tb),
        grid_spec=pltpu.PrefetchScalarGridSpec(
            num_scalar_prefetch=1,
            grid=(m // tb,),
            in_specs=[pl.BlockSpec((tb, nt, LANES), lambda i, pos: (i, 0, 0)), pl.BlockSpec(memory_space=pl.ANY)],
            out_specs=pl.BlockSpec(memory_space=pl.ANY),
            scratch_shapes=[pltpu.SemaphoreType.DMA(())],
        ),
        out_shape=jax.ShapeDtypeStruct(xs.shape, xs.dtype),
        input_output_aliases={2: 0},
        compiler_params=_cparams(("arbitrary",)),
        name="dispatch",
    )(pos_flat, hp, xs)


def _experts_body(e_ref, xb_ref, valid_ref, x_ref, w1g_ref, w1u_ref, b1g_ref, b1u_ref, w2_ref, b2_ref,
                  y_ref, act_ref, stage_ref, *, n_fa, n_jb):
    del e_ref, xb_ref
    i = pl.program_id(0)
    s = pl.program_id(1)
    valid = valid_ref[i]
    _, rb, oc = stage_ref.shape
    d = n_jb * oc
    nt = d // LANES
    n_sub = rb // SUB_TILE
    nv = (valid + SUB_TILE - 1) // SUB_TILE

    @pl.when(s == 0)
    def _():
        for t in range(n_sub):
            rows = slice(t * SUB_TILE, (t + 1) * SUB_TILE)

            @pl.when(t < nv)
            def _():
                x2d = x_ref[rows].reshape(SUB_TILE, d)
                for j in range(n_jb):
                    stage_ref[j, rows, :] = x2d[:, j * oc:(j + 1) * oc]

            @pl.when(t >= nv)
            def _():
                y_ref[rows] = jnp.zeros((SUB_TILE, nt, LANES), BF16)

    for v in range(1, n_sub + 1):
        m = v * SUB_TILE

        @pl.when((nv == v) & (s < n_fa))
        def _():
            xs = jnp.concatenate([stage_ref[j, 0:m, :] for j in range(n_jb)], axis=1)
            gate = jnp.minimum(_dot(xs, w1g_ref[...].astype(BF16)) + b1g_ref[...], SWIGLU_LIMIT)
            up = jnp.clip(_dot(xs, w1u_ref[...].astype(BF16)) + b1u_ref[...], -SWIGLU_LIMIT, SWIGLU_LIMIT)
            act = (up + 1.0) * gate * _sigmoid(SWIGLU_ALPHA * gate)
            act_ref[s, 0:m, :] = act.astype(BF16)

        @pl.when((nv == v) & (s >= n_fa))
        def _():
            a = jnp.concatenate([act_ref[f, 0:m, :] for f in range(n_fa)], axis=1)
            out = (_dot(a, w2_ref[...].astype(BF16)) + b2_ref[...]).astype(BF16)
            stage_ref[s - n_fa, 0:m, :] = out

        @pl.when((nv == v) & (s == n_fa + n_jb - 1))
        def _():
            full = jnp.concatenate([stage_ref[j, 0:m, :] for j in range(n_jb)], axis=1)
            y_ref[0:m] = full.reshape(m, nt, LANES)


def _experts(item_e, item_xb, item_valid, xs, w1, b1, w2, b2):
    n_items = item_e.shape[0]
    n_exp, d, ff2 = w1.shape
    ff = ff2 // 2
    nt = d // LANES
    rb = ROW_BLOCK
    n_fa = ff // FF_CHUNK
    n_jb = d // OUT_CHUNK

    def fa(i, s, v):
        return jnp.where(v[i] > 0, jnp.minimum(s, n_fa - 1), n_fa - 1)

    def jb(i, s, v):
        return jnp.where(v[i] > 0, jnp.maximum(s - n_fa, 0), n_jb - 1)

    return pl.pallas_call(
        functools.partial(_experts_body, n_fa=n_fa, n_jb=n_jb),
        grid_spec=pltpu.PrefetchScalarGridSpec(
            num_scalar_prefetch=3,
            grid=(n_items, n_fa + n_jb),
            in_specs=[
                pl.BlockSpec((rb, nt, LANES), lambda i, s, e, xb, v: (xb[i], 0, 0)),
                pl.BlockSpec((None, d, FF_CHUNK), lambda i, s, e, xb, v: (e[i], 0, fa(i, s, v))),
                pl.BlockSpec((None, d, FF_CHUNK), lambda i, s, e, xb, v: (e[i], 0, n_fa + fa(i, s, v))),
                pl.BlockSpec((None, 1, FF_CHUNK), lambda i, s, e, xb, v: (e[i], 0, fa(i, s, v))),
                pl.BlockSpec((None, 1, FF_CHUNK), lambda i, s, e, xb, v: (e[i], 0, n_fa + fa(i, s, v))),
                pl.BlockSpec((None, ff, OUT_CHUNK), lambda i, s, e, xb, v: (e[i], 0, jb(i, s, v))),
                pl.BlockSpec((None, 1, OUT_CHUNK), lambda i, s, e, xb, v: (e[i], 0, jb(i, s, v))),
            ],
            out_specs=pl.BlockSpec((rb, nt, LANES), lambda i, s, e, xb, v: (i, 0, 0)),
            scratch_shapes=[pltpu.VMEM((n_fa, rb, FF_CHUNK), BF16), pltpu.VMEM((n_jb, rb, OUT_CHUNK), BF16)],
        ),
        out_shape=jax.ShapeDtypeStruct((n_items * rb, nt, LANES), BF16),
        compiler_params=_cparams(("arbitrary", "arbitrary")),
        name="experts",
    )(item_e, item_xb, item_valid, xs, w1, w1, b1.reshape(n_exp, 1, ff2), b1.reshape(n_exp, 1, ff2),
      w2, b2.reshape(n_exp, 1, d))


def _combine_body(pos_ref, route_ref, x2_ref, mod_ref, g_ref, y_hbm, o_ref, buf_ref, sem, *, tb, d):
    i = pl.program_id(0)
    n_steps = pl.num_programs(0)
    n = tb * TOP_K

    def issue_block(blk, slot):
        def issue(r, carry):
            for k in range(TOP_K):
                src = y_hbm.at[pos_ref[blk * n + r * TOP_K + k]]
                pltpu.make_async_copy(src, buf_ref.at[slot, k, r], sem.at[slot]).start()
            return carry

        lax.fori_loop(0, tb, issue, 0, unroll=4)

    @pl.when(i == 0)
    def _():
        issue_block(0, 0)

    @pl.when(i + 1 < n_steps)
    def _():
        issue_block(i + 1, (i + 1) % 2)

    slot = i % 2
    for k in range(TOP_K):
        pltpu.make_async_copy(y_hbm.at[pl.ds(0, tb)], buf_ref.at[slot, k], sem.at[slot]).wait()

    route = route_ref[...]
    moe = jnp.zeros((tb, d), F32)
    for k in range(TOP_K):
        p_k = lax.bitcast_convert_type(route[:, 2 * TOP_K + k:2 * TOP_K + k + 1], F32)
        moe = moe + p_k * buf_ref[slot, k].reshape(tb, d).astype(F32)
    x = x2_ref[...] + mod_ref[:, 5 * d:6 * d] * moe
    ms = jnp.mean(x * x, axis=-1, keepdims=True)
    o_ref[...] = x * lax.rsqrt(ms + NORM_EPS) * g_ref[...]


def _combine(pos_flat, route, x2, mod3, mod_base, tiles_per_mod, final_g, y_sorted, tb):
    m, d = x2.shape
    return pl.pallas_call(
        functools.partial(_combine_body, tb=tb, d=d),
        grid_spec=pltpu.PrefetchScalarGridSpec(
            num_scalar_prefetch=1,
            grid=(m // tb,),
            in_specs=[
                pl.BlockSpec((tb, LANES), lambda i, pos: (i, 0)),
                pl.BlockSpec((tb, d), lambda i, pos: (i, 0)),
                pl.BlockSpec((None, 1, mod3.shape[2]), lambda i, pos: (mod_base + i // tiles_per_mod, 0, 0)),
                pl.BlockSpec((1, d), lambda i, pos: (0, 0)),
                pl.BlockSpec(memory_space=pl.ANY),
            ],
            out_specs=pl.BlockSpec((tb, d), lambda i, pos: (i, 0)),
            scratch_shapes=[pltpu.VMEM((2, TOP_K, tb, d // LANES, LANES), BF16), pltpu.SemaphoreType.DMA((2,))],
        ),
        out_shape=jax.ShapeDtypeStruct((m, d), F32),
        compiler_params=_cparams(("arbitrary",)),
        name="combine",
    )(pos_flat, route, x2, mod3, final_g.reshape(1, d), y_sorted)


def _is_pow2(v):
    return v > 0 and (v & (v - 1)) == 0


def kernel(x_prompt, x_sample, state_gla_fwd, state_gla_bwd, c, c_ctx, norm1_g, w_ada, b_ada, w_in, w_gk_up, b_gk,
           gla_norm_g, w_gla_out, conv_w, w_conv_out, w_o, norm2_g, w_router, b_router, w_exp1, b_exp1, w_exp2,
           b_exp2, final_norm_g):
    batch, seq, d = x_prompt.shape
    dec_batch, dec_seq, _ = x_sample.shape
    depth = norm1_g.shape[0]
    assert depth == 1, "single trunk layer"
    h = GLA_HEADS
    dk_all = w_gk_up.shape[-1]
    dk, dv = dk_all // h, d // h
    n_exp = w_router.shape[-1]
    grid_w = 64
    assert _is_pow2(seq) and _is_pow2(grid_w) and dec_seq % grid_w == 0
    tm = 256
    tm_mix = 512
    assert tm % seq == 0 and tm % grid_w == 0, "conv segments must not straddle token tiles"
    assert tm_mix % tm == 0 and seq * batch % tm_mix == 0 and dec_seq % tm_mix == 0

    c_gk = 2 * dk_all + 2 * d
    w_in0 = w_in[0]
    w_att = w_in0[:, :c_gk].astype(BF16)
    w_rest = w_in0[:, c_gk + 2 * GK_RANK:].astype(BF16)
    w_gk = w_in0[:, c_gk:c_gk + 2 * GK_RANK]
    zr = jnp.zeros((GK_RANK, dk_all), F32)
    w_up2 = jnp.stack([jnp.concatenate([w_gk_up[0, 0], zr], axis=0), jnp.concatenate([zr, w_gk_up[0, 1]], axis=0)])
    w_gla_out_b = w_gla_out[0].astype(BF16)
    w_conv_out_b = w_conv_out[0].astype(BF16)
    w_o_b = w_o[0].astype(BF16)

    n_cond = 1 + dec_batch
    cond = jnp.concatenate([c_ctx[None, :], c, jnp.zeros((8 - n_cond % 8, d), F32)], axis=0)
    mod = _ada(cond, w_ada[0], b_ada[0])
    mod3 = mod.reshape(mod.shape[0], 1, mod.shape[1])


    passes = []
    counts = jnp.zeros((1, n_exp), F32)
    for name, x3, n_seq, t, mod_base, seg in (("ctx", x_prompt, batch, seq, 0, seq),
                                              ("lat", x_sample, dec_batch, dec_seq, 1, grid_w)):
        m = n_seq * t
        x = x3.reshape(m, d)
        tiles_per_mod = (m // tm) if name == "ctx" else (t // tm)
        hb, gk_lo = _pre(x, mod3, mod_base, tiles_per_mod, norm1_g[0], w_gk, tm)
        z_att = _matmul(hb, w_att, min(m, 2048), 1024)
        z_rest = _matmul(hb, w_rest, min(m, 2048), 1024)
        if name == "ctx":
            yg, s_f, s_b = _gla(z_att, gk_lo, w_up2, b_gk[0], gla_norm_g[0], n_seq, t, dk, dv, emit_state=True)
        else:
            (yg,) = _gla(z_att, gk_lo, w_up2, b_gk[0], gla_norm_g[0], n_seq, t, dk, dv,
                         s0f=state_gla_fwd[:, 0], s0b=state_gla_bwd[:, 0])
            s_f = s_b = None
        merged = _mix(yg, z_rest, conv_w[0], w_gla_out_b, w_conv_out_b, tm_mix, seg, d, 0)
        x2, hp, route, counts = _route(merged, x, mod3, mod_base, tiles_per_mod * tm // tm_mix, w_o_b, norm2_g[0],
                                       w_router[0], b_router[0], counts, tm_mix)
        passes.append(dict(x2=x2, hp=hp, route=route, s_f=s_f, s_b=s_b, mod_base=mod_base,
                           tiles_per_mod=tiles_per_mod, shape=x3.shape))

    n_tok = sum(p["x2"].shape[0] for p in passes)
    n_items = (n_tok * TOP_K) // ROW_BLOCK + n_exp
    cnt = counts[0].astype(I32)
    nblk = (cnt + ROW_BLOCK - 1) // ROW_BLOCK
    blk_end = jnp.cumsum(nblk)
    blk0 = blk_end - nblk
    total = blk_end[-1]
    item = jnp.arange(n_items, dtype=I32)
    owner = lambda blk: jnp.sum((blk_end[None, :] <= blk[:, None]).astype(I32), axis=1)
    live = item < total
    item_e = jnp.where(live, jnp.minimum(owner(item), n_exp - 1), owner(total[None] - 1)[0])
    item_xb = jnp.where(live, item, total - 1)
    item_valid = jnp.where(live, jnp.clip(cnt[item_e] - (item - blk0[item_e]) * ROW_BLOCK, 0, ROW_BLOCK), 0)
    slot0 = blk0 * ROW_BLOCK

    xs = jnp.zeros((n_items * ROW_BLOCK, d // LANES, LANES), BF16)
    for p in passes:
        e_idx = p["route"][:, 0:TOP_K]
        rank = p["route"][:, TOP_K:2 * TOP_K]
        p["pos"] = (slot0[e_idx] + rank).reshape(-1)
        xs = _dispatch(p["pos"], p["hp"], xs, tm)

    y_sorted = _experts(item_e, item_xb, item_valid, xs, w_exp1[0], b_exp1[0], w_exp2[0], b_exp2[0])

    outs = []
    for p in passes:
        y = _combine(p["pos"], p["route"], p["x2"], mod3, p["mod_base"], p["tiles_per_mod"], final_norm_g,
                     y_sorted, tm)
        outs.append(y.reshape(p["shape"]))

    st_shape = (batch, depth, h, dk, dv)
    return (outs[0], outs[1], passes[0]["s_f"].reshape(st_shape), passes[0]["s_b"].reshape(st_shape))
```

```python
import functools

import jax
import jax.numpy as jnp
from jax import lax
from jax.experimental import pallas as pl
from jax.experimental.pallas import tpu as pltpu

F32 = jnp.float32
BF16 = jnp.bfloat16
I32 = jnp.int32

GLA_HEADS = 4
GK_RANK = 16
GATE_NORMALIZER = 16.0
TOP_K = 4
SWIGLU_ALPHA = 1.702
SWIGLU_LIMIT = 7.0
NORM_EPS = 1e-6

LANES = 128
BF16_ROWS = 16
VMEM_LIMIT = 56 * 1024 * 1024

GLA_CHUNK = 64
GLA_SUB = BF16_ROWS
GLA_UNROLL = 4
NEG_BIG = -1e30

ROW_BLOCK = 1024
SUB_TILE = 256
FF_CHUNK = 512
OUT_CHUNK = 512


def _cparams(sem):
    return pltpu.CompilerParams(dimension_semantics=sem, vmem_limit_bytes=VMEM_LIMIT)


def _dot(a, b):
    return jnp.dot(a, b, preferred_element_type=F32)


def _dot_nt(a, b):
    return lax.dot_general(a, b, (((1,), (1,)), ((), ())), preferred_element_type=F32)


def _dot_tn(a, b):
    return lax.dot_general(a, b, (((0,), (0,)), ((), ())), preferred_element_type=F32)


def _split_bf16(a, n):
    parts = []
    r = a
    for _ in range(n):
        p = r.astype(BF16)
        parts.append(p)
        r = r - p.astype(F32)
    return parts


def _dot_f32(a, b):
    a_hi, a_lo = _split_bf16(a, 2)
    b_hi, b_lo = _split_bf16(b, 2)
    return _dot(a_hi, b_hi) + (_dot(a_lo, b_hi) + _dot(a_hi, b_lo))


def _sigmoid(x):
    return 1.0 / (1.0 + jnp.exp(-x))


def _resident(shape):
    return pl.BlockSpec(shape, lambda i: (0,) * len(shape), pipeline_mode=pl.Buffered(1))


def _ada_body(c_ref, w_ref, b_ref, o_ref):
    c = c_ref[...]
    s = c * _sigmoid(c)
    o_ref[...] = _dot(s.astype(BF16), w_ref[...].astype(BF16)) + b_ref[...]


def _ada(cond, w_ada, b_ada):
    rows, d = cond.shape
    n = w_ada.shape[1]
    bn = 1024
    return pl.pallas_call(
        _ada_body,
        grid=(n // bn,),
        in_specs=[
            pl.BlockSpec((rows, d), lambda j: (0, 0)),
            pl.BlockSpec((d, bn), lambda j: (0, j)),
            pl.BlockSpec((1, bn), lambda j: (0, j)),
        ],
        out_specs=pl.BlockSpec((rows, bn), lambda j: (0, j)),
        out_shape=jax.ShapeDtypeStruct((rows, n), F32),
        compiler_params=_cparams(("arbitrary",)),
        name="ada",
    )(cond, w_ada, b_ada.reshape(1, n))


def _pre_body(x_ref, mod_ref, g_ref, wgk_ref, h_ref, gk_ref, *, d):
    x = x_ref[...]
    ms = jnp.mean(x * x, axis=-1, keepdims=True)
    y = x * lax.rsqrt(ms + NORM_EPS) * g_ref[...]
    h = y * (1.0 + mod_ref[:, d:2 * d]) + mod_ref[:, 0:d]
    h_ref[...] = h.astype(BF16)
    gk_ref[...] = _dot_f32(h, wgk_ref[...])


def _pre(x, mod3, mod_base, tiles_per_mod, norm_g, w_gk, tm):
    m, d = x.shape
    r = w_gk.shape[1]
    return pl.pallas_call(
        functools.partial(_pre_body, d=d),
        grid=(m // tm,),
        in_specs=[
            pl.BlockSpec((tm, d), lambda i: (i, 0)),
            pl.BlockSpec((None, 1, mod3.shape[2]), lambda i: (mod_base + i // tiles_per_mod, 0, 0)),
            pl.BlockSpec((1, d), lambda i: (0, 0)),
            pl.BlockSpec((d, r), lambda i: (0, 0)),
        ],
        out_specs=[
            pl.BlockSpec((tm, d), lambda i: (i, 0)),
            pl.BlockSpec((tm, r), lambda i: (i, 0)),
        ],
        out_shape=[jax.ShapeDtypeStruct((m, d), BF16), jax.ShapeDtypeStruct((m, r), F32)],
        compiler_params=_cparams(("arbitrary",)),
        name="pre",
    )(x, mod3, norm_g.reshape(1, d), w_gk)


def _colcast_body(a_ref, b_ref, o_ref, *, shift):
    a = a_ref[...]
    if shift:
        a = jnp.concatenate([a[:, shift:], b_ref[:, :shift]], axis=1)
    o_ref[...] = a.astype(BF16)


def _colcast(w, col0, n_cols, bn):
    k = w.shape[0]
    base, shift = divmod(col0, bn)
    assert n_cols % bn == 0 and col0 + n_cols <= w.shape[1]
    return pl.pallas_call(
        functools.partial(_colcast_body, shift=shift),
        grid=(n_cols // bn,),
        in_specs=[pl.BlockSpec((k, bn), lambda j: (0, base + j)),
                  pl.BlockSpec((k, bn), lambda j: (0, base + j + (1 if shift else 0)))],
        out_specs=pl.BlockSpec((k, bn), lambda j: (0, j)),
        out_shape=jax.ShapeDtypeStruct((k, n_cols), BF16),
        compiler_params=_cparams(("arbitrary",)),
        name="colcast",
    )(w, w)


def _mm_body(a_ref, b_ref, o_ref):
    o_ref[...] = _dot(a_ref[...], b_ref[...]).astype(o_ref.dtype)


def _matmul(a, b, bm, bn):
    m, k = a.shape
    n = b.shape[1]
    while n % bn:
        bn //= 2
    assert bn % LANES == 0
    return pl.pallas_call(
        _mm_body,
        grid=(m // bm, n // bn),
        in_specs=[
            pl.BlockSpec((bm, k), lambda i, j: (i, 0)),
            pl.BlockSpec((k, bn), lambda i, j: (0, j)),
        ],
        out_specs=pl.BlockSpec((bm, bn), lambda i, j: (i, j)),
        out_shape=jax.ShapeDtypeStruct((m, n), BF16),
        compiler_params=_cparams(("arbitrary", "arbitrary")),
        name="in_proj",
    )(a, b)


def _gla_prep(q, k, v, gk_lo, w_up, b_up, reverse):
    c, dk = q.shape
    gk = _dot_f32(gk_lo, w_up) + b_up
    g = (jnp.minimum(gk, 0.0) - jnp.log(1.0 + jnp.exp(-jnp.abs(gk)))) * (1.0 / GATE_NORMALIZER)
    row = lax.broadcasted_iota(I32, (c, c), 0)
    col = lax.broadcasted_iota(I32, (c, c), 1)
    tri = (col >= row) if reverse else (col <= row)
    tri_b = jnp.where(tri, 1.0, 0.0).astype(BF16)
    g1, g2 = _split_bf16(g, 2)
    b = _dot(tri_b, g1) + _dot(tri_b, g2)

    qf = q.astype(F32) * (dk ** -0.5)
    kf = k.astype(F32)
    q_in = (qf * jnp.exp(b)).astype(BF16)

    rowv = lax.broadcasted_iota(I32, (c, 1), 0)
    lhs, rhs = [], []
    for j in range(c // GLA_SUB):
        lo, hi = j * GLA_SUB, (j + 1) * GLA_SUB
        e_row = lo if reverse else hi - 1
        e = b[e_row:e_row + 1, :]
        in_blk = (rowv >= lo) & (rowv < hi)
        active = (rowv < hi) if reverse else (rowv >= lo)
        lhs.append((qf * jnp.exp(jnp.where(active, b - e, NEG_BIG))).astype(BF16))
        rhs.append((kf * jnp.exp(jnp.where(in_blk, e - b, NEG_BIG))).astype(BF16))
    p = _dot_nt(jnp.concatenate(lhs, axis=1), jnp.concatenate(rhs, axis=1))
    p = jnp.where(tri, p, 0.0)
    o_intra = _dot(p.astype(BF16), v)

    l_row = 0 if reverse else c - 1
    bl = b[l_row:l_row + 1, :]
    ks = (kf * jnp.exp(bl - b)).astype(BF16)
    return q_in, ks, jnp.exp(bl), o_intra


def _gla_state_step(prep, v, st_ref):
    q_in, ks, decay, o_intra = prep
    st = st_ref[...]
    o = o_intra + _dot_nt(q_in, st.astype(BF16))
    st_ref[...] = st * decay + _dot_tn(v, ks)
    return o


def _gla_body(*refs, t, unroll, has_init, emit_state):
    q_ref, k_ref, v_ref, og_ref, gk_ref, wup_ref, bup_ref, gn_ref = refs[:8]
    pos = 8
    if has_init:
        s0f_ref, s0b_ref = refs[pos:pos + 2]
        pos += 2
    y_ref = refs[pos]
    pos += 1
    if emit_state:
        sf_ref, sb_ref = refs[pos:pos + 2]
        pos += 2
    o_ref, stf_ref, stb_ref = refs[pos:pos + 3]

    c = GLA_CHUNK
    n = t // c
    if has_init:
        stf_ref[...] = s0f_ref[...].T
        stb_ref[...] = s0b_ref[...].T
    else:
        stf_ref[...] = jnp.zeros_like(stf_ref)
        stb_ref[...] = jnp.zeros_like(stb_ref)

    wup_f, wup_b = wup_ref[0], wup_ref[1]
    bup_f, bup_b = bup_ref[0], bup_ref[1]

    def rows(r):
        return pl.ds(pl.multiple_of(r * c, c), c)

    def make_step(first_touch):
        def step(i, carry):
            jobs = []
            for u in range(unroll):
                cf = i * unroll + u
                jobs.append((rows(cf), wup_f, bup_f, stf_ref, False))
                jobs.append((rows(n - 1 - cf), wup_b, bup_b, stb_ref, True))
            preps = [_gla_prep(q_ref[r, :], k_ref[r, :], v_ref[r, :], gk_ref[r, :], w, bias, rev)
                     for r, w, bias, _, rev in jobs]
            for (r, _, _, st_ref, _), prep in zip(jobs, preps):
                o = _gla_state_step(prep, v_ref[r, :], st_ref)
                if first_touch:
                    o_ref[r, :] = o
                else:
                    o_ref[r, :] += o
            return carry
        return step

    half = n // (2 * unroll)
    lax.fori_loop(0, half, make_step(True), 0)
    lax.fori_loop(half, 2 * half, make_step(False), 0)

    if emit_state:
        sf_ref[...] = stf_ref[...].T
        sb_ref[...] = stb_ref[...].T

    blk = min(t, 256)

    def fin(i, carry):
        r = pl.multiple_of(i * blk, blk)
        o = o_ref[pl.ds(r, blk), :]
        ms = jnp.mean(o * o, axis=-1, keepdims=True)
        og = og_ref[pl.ds(r, blk), :].astype(F32)
        y = o * lax.rsqrt(ms + NORM_EPS) * gn_ref[...] * (og * _sigmoid(og))
        y_ref[pl.ds(r, blk), :] = y.astype(BF16)
        return carry

    lax.fori_loop(0, t // blk, fin, 0)


def _gla(z, gk_lo, w_gk_up, b_gk, gn, n_seq, t, dk, dv, s0f=None, s0b=None, emit_state=False):
    h = GLA_HEADS
    m = n_seq * t
    n_chunks = t // GLA_CHUNK
    unroll = max(1, min(GLA_UNROLL, n_chunks // 2))
    assert t % (2 * unroll * GLA_CHUNK) == 0
    has_init = s0f is not None
    qk_blocks = (h * dk) // dk
    in_specs = [
        pl.BlockSpec((t, dk), lambda s, hh: (s, hh)),
        pl.BlockSpec((t, dk), lambda s, hh: (s, qk_blocks + hh)),
        pl.BlockSpec((t, dv), lambda s, hh: (s, (2 * h * dk) // dv + hh)),
        pl.BlockSpec((t, dv), lambda s, hh: (s, (2 * h * dk) // dv + h + hh)),
        pl.BlockSpec((t, 2 * GK_RANK), lambda s, hh: (s, 0)),
        pl.BlockSpec((2, 2 * GK_RANK, dk), lambda s, hh: (0, 0, hh)),
        pl.BlockSpec((2, 1, dk), lambda s, hh: (0, 0, hh)),
        pl.BlockSpec((1, dv), lambda s, hh: (0, 0)),
    ]
    args = [z, z, z, z, gk_lo, w_gk_up, b_gk.reshape(2, 1, h * dk), gn.reshape(1, dv)]
    if has_init:
        in_specs += [pl.BlockSpec((None, dk, dv), lambda s, hh: (s * h + hh, 0, 0))] * 2
        args += [s0f.reshape(n_seq * h, dk, dv), s0b.reshape(n_seq * h, dk, dv)]
    out_specs = [pl.BlockSpec((t, dv), lambda s, hh: (s, hh))]
    out_shape = [jax.ShapeDtypeStruct((m, h * dv), BF16)]
    if emit_state:
        out_specs += [pl.BlockSpec((None, dk, dv), lambda s, hh: (s * h + hh, 0, 0))] * 2
        out_shape += [jax.ShapeDtypeStruct((n_seq * h, dk, dv), F32)] * 2
    return pl.pallas_call(
        functools.partial(_gla_body, t=t, unroll=unroll, has_init=has_init, emit_state=emit_state),
        grid=(n_seq, h),
        in_specs=in_specs,
        out_specs=out_specs,
        out_shape=out_shape,
        scratch_shapes=[pltpu.VMEM((t, dv), F32), pltpu.VMEM((dv, dk), F32), pltpu.VMEM((dv, dk), F32)],
        compiler_params=_cparams(("arbitrary", "arbitrary")),
        name="gla",
    )(*args)


def _mix_body(yg_ref, cx_ref, cb_ref, cc_ref, ga_ref, gb_ref, cw_ref, wg_ref, wc_ref, o_ref, *, seg):
    tm = yg_ref.shape[0]
    s = cc_ref[...].astype(F32) * cx_ref[...].astype(F32)
    rowv = lax.broadcasted_iota(I32, (tm, 1), 0)
    pos = rowv & (seg - 1)
    prev = jnp.where(pos == 0, 0.0, pltpu.roll(s, 1, axis=0))
    nxt = jnp.where(pos == seg - 1, 0.0, pltpu.roll(s, tm - 1, axis=0))
    conv = prev * cw_ref[0:1, :] + s * cw_ref[1:2, :] + nxt * cw_ref[2:3, :]
    u = (cb_ref[...].astype(F32) * conv).astype(BF16)
    y_gla = _dot(yg_ref[...], wg_ref[...])
    y_conv = _dot(u, wc_ref[...])
    merged = _sigmoid(ga_ref[...].astype(F32)) * y_gla + _sigmoid(gb_ref[...].astype(F32)) * y_conv
    o_ref[...] = merged.astype(BF16)


def _mix(yg, z, conv_w, w_gla_out, w_conv_out, tm, seg, d, col0):
    m = yg.shape[0]
    zspec = lambda c: pl.BlockSpec((tm, d), lambda i: (i, col0 + c))
    const = lambda shape: pl.BlockSpec(shape, lambda i: (0, 0))
    return pl.pallas_call(
        functools.partial(_mix_body, seg=seg),
        grid=(m // tm,),
        in_specs=[pl.BlockSpec((tm, d), lambda i: (i, 0)), zspec(0), zspec(1), zspec(2), zspec(3), zspec(4),
                  const(conv_w.shape), _resident(w_gla_out.shape), _resident(w_conv_out.shape)],
        out_specs=pl.BlockSpec((tm, d), lambda i: (i, 0)),
        out_shape=jax.ShapeDtypeStruct((m, d), BF16),
        compiler_params=_cparams(("arbitrary",)),
        name="mix",
    )(yg, z, z, z, z, z, conv_w, w_gla_out, w_conv_out)


def _route_body(mg_ref, x_ref, mod_ref, wo_ref, g_ref, wr_ref, br_ref, cnt_in_ref,
                x2_ref, hp_ref, route_ref, cnt_ref, carry_ref, *, d, n_exp):
    i = pl.program_id(0)
    tm = x_ref.shape[0]

    @pl.when(i == 0)
    def _():
        carry_ref[...] = cnt_in_ref[...]

    x2 = x_ref[...] + mod_ref[:, 2 * d:3 * d] * _dot(mg_ref[...], wo_ref[...])
    x2_ref[...] = x2
    ms = jnp.mean(x2 * x2, axis=-1, keepdims=True)
    h = x2 * lax.rsqrt(ms + NORM_EPS) * g_ref[...] * (1.0 + mod_ref[:, 4 * d:5 * d]) + mod_ref[:, 3 * d:4 * d]
    hp_ref[...] = h.astype(BF16).reshape(tm, d // LANES, LANES)

    logits = _dot_f32(h, wr_ref[...]) + br_ref[...]
    lane = lax.broadcasted_iota(I32, (tm, n_exp), 1).astype(F32)
    work = logits
    vals, hots = [], []
    for _ in range(TOP_K):
        mx = jnp.max(work, axis=-1, keepdims=True)
        idx = jnp.min(jnp.where(work == mx, lane, float(n_exp)), axis=-1, keepdims=True)
        hot = lane == idx
        vals.append(mx)
        hots.append(hot)
        work = jnp.where(hot, -jnp.inf, work)
    exps = [jnp.exp(v - vals[0]) for v in vals]
    inv = 1.0 / (exps[0] + exps[1] + exps[2] + exps[3])

    cnt = jnp.zeros((tm, n_exp), F32)
    for hot in hots:
        cnt = cnt + jnp.where(hot, 1.0, 0.0)
    r = lax.broadcasted_iota(I32, (tm, tm), 0)
    cc = lax.broadcasted_iota(I32, (tm, tm), 1)
    strict = jnp.where(cc < r, 1.0, 0.0).astype(BF16)
    before = _dot(strict, cnt.astype(BF16)) + carry_ref[...]
    carry_ref[...] = carry_ref[...] + jnp.sum(cnt, axis=0, keepdims=True)

    out_lane = lax.broadcasted_iota(I32, (tm, LANES), 1)
    route = jnp.zeros((tm, LANES), I32)
    for k in range(TOP_K):
        e_k = jnp.sum(jnp.where(hots[k], lane, 0.0), axis=-1, keepdims=True).astype(I32)
        rank_k = jnp.sum(jnp.where(hots[k], before, 0.0), axis=-1, keepdims=True).astype(I32)
        p_k = lax.bitcast_convert_type(exps[k] * inv, I32)
        route = jnp.where(out_lane == k, e_k, route)
        route = jnp.where(out_lane == TOP_K + k, rank_k, route)
        route = jnp.where(out_lane == 2 * TOP_K + k, p_k, route)
    route_ref[...] = route

    @pl.when(i == pl.num_programs(0) - 1)
    def _():
        cnt_ref[...] = carry_ref[...]


def _route(merged, x, mod3, mod_base, tiles_per_mod, w_o, norm_g, w_router, b_router, counts_in, tm):
    m, d = x.shape
    n_exp = w_router.shape[1]
    const = lambda shape: pl.BlockSpec(shape, lambda i: (0, 0))
    tile = lambda w: pl.BlockSpec((tm, w), lambda i: (i, 0))
    return pl.pallas_call(
        functools.partial(_route_body, d=d, n_exp=n_exp),
        grid=(m // tm,),
        in_specs=[tile(d), tile(d),
                  pl.BlockSpec((None, 1, mod3.shape[2]), lambda i: (mod_base + i // tiles_per_mod, 0, 0)),
                  _resident(w_o.shape), const((1, d)), const(w_router.shape), const((1, n_exp)),
                  const((1, n_exp))],
        out_specs=[tile(d), pl.BlockSpec((tm, d // LANES, LANES), lambda i: (i, 0, 0)), tile(LANES),
                   const((1, n_exp))],
        out_shape=[jax.ShapeDtypeStruct((m, d), F32), jax.ShapeDtypeStruct((m, d // LANES, LANES), BF16),
                   jax.ShapeDtypeStruct((m, LANES), I32), jax.ShapeDtypeStruct((1, n_exp), F32)],
        scratch_shapes=[pltpu.VMEM((1, n_exp), F32)],
        compiler_params=_cparams(("arbitrary",)),
        name="route",
    )(merged, x, mod3, w_o, norm_g.reshape(1, d), w_router, b_router.reshape(1, n_exp), counts_in)


def _dispatch_body(pos_ref, zero_ref, hp_a_ref, hp_b_ref, xs_ref, sem, zbuf_ref, *, tb, n_a):
    i = pl.program_id(0)
    n = tb * TOP_K

    @pl.when(i == 0)
    def _():
        zbuf_ref[...] = jnp.zeros_like(zbuf_ref)
        n_gran = zero_ref.shape[0]

        def fill(g, carry):
            @pl.when(zero_ref[g] != 0)
            def _():
                pltpu.make_async_copy(zbuf_ref, xs_ref.at[pl.ds(g * SUB_TILE, SUB_TILE)], sem).start()
            return carry

        def drain(g, carry):
            @pl.when(zero_ref[g] != 0)
            def _():
                pltpu.make_async_copy(zbuf_ref, xs_ref.at[pl.ds(0, SUB_TILE)], sem).wait()
            return carry

        lax.fori_loop(0, n_gran, fill, 0)
        lax.fori_loop(0, n_gran, drain, 0)

    def scatter(hp_ref):
        def issue(r, carry):
            for k in range(TOP_K):
                slot = pos_ref[i * n + r * TOP_K + k]
                pltpu.make_async_copy(hp_ref.at[r], xs_ref.at[slot], sem).start()
            return carry

        lax.fori_loop(0, tb, issue, 0, unroll=4)
        for k in range(TOP_K):
            pltpu.make_async_copy(hp_ref, xs_ref.at[pl.ds(0, tb)], sem).wait()

    @pl.when(i < n_a)
    def _():
        scatter(hp_a_ref)

    @pl.when(i >= n_a)
    def _():
        scatter(hp_b_ref)


def _dispatch(pos_flat, zero_flags, hp_a, hp_b, n_slots, tb):
    nt = hp_a.shape[1]
    n_a, n_b = hp_a.shape[0] // tb, hp_b.shape[0] // tb
    return pl.pallas_call(
        functools.partial(_dispatch_body, tb=tb, n_a=n_a),
        grid_spec=pltpu.PrefetchScalarGridSpec(
            num_scalar_prefetch=2,
            grid=(n_a + n_b,),
            in_specs=[
                pl.BlockSpec((tb, nt, LANES), lambda i, pos, zf: (jnp.minimum(i, n_a - 1), 0, 0)),
                pl.BlockSpec((tb, nt, LANES), lambda i, pos, zf: (jnp.maximum(i - n_a, 0), 0, 0)),
            ],
            out_specs=pl.BlockSpec(memory_space=pl.ANY),
            scratch_shapes=[pltpu.SemaphoreType.DMA(()), pltpu.VMEM((SUB_TILE, nt, LANES), BF16)],
        ),
        out_shape=jax.ShapeDtypeStruct((n_slots, nt, LANES), BF16),
        compiler_params=_cparams(("arbitrary",)),
        name="dispatch",
    )(pos_flat, zero_flags, hp_a, hp_b)


def _experts_body(e_ref, xb_ref, valid_ref, x_ref, w1g_ref, w1u_ref, b1g_ref, b1u_ref, w2_ref, b2_ref,
                  y_ref, act_ref, stage_ref, *, n_fa, n_jb):
    del e_ref, xb_ref
    i = pl.program_id(0)
    s = pl.program_id(1)
    valid = valid_ref[i]
    _, rb, oc = stage_ref.shape
    d = n_jb * oc
    nt = d // LANES
    n_sub = rb // SUB_TILE
    nv = (valid + SUB_TILE - 1) // SUB_TILE

    @pl.when(s == 0)
    def _():
        for t in range(n_sub):
            rows = slice(t * SUB_TILE, (t + 1) * SUB_TILE)

            @pl.when(t < nv)
            def _():
                x2d = x_ref[rows].reshape(SUB_TILE, d)
                for j in range(n_jb):
                    stage_ref[j, rows, :] = x2d[:, j * oc:(j + 1) * oc]

            @pl.when(t >= nv)
            def _():
                y_ref[rows] = jnp.zeros((SUB_TILE, nt, LANES), BF16)

    for v in range(1, n_sub + 1):
        m = v * SUB_TILE

        @pl.when((nv == v) & (s < n_fa))
        def _():
            xs = jnp.concatenate([stage_ref[j, 0:m, :] for j in range(n_jb)], axis=1)
            gate = jnp.minimum(_dot(xs, w1g_ref[...].astype(BF16)) + b1g_ref[...], SWIGLU_LIMIT)
            up = jnp.clip(_dot(xs, w1u_ref[...].astype(BF16)) + b1u_ref[...], -SWIGLU_LIMIT, SWIGLU_LIMIT)
            act = (up + 1.0) * gate * _sigmoid(SWIGLU_ALPHA * gate)
            act_ref[s, 0:m, :] = act.astype(BF16)

        @pl.when((nv == v) & (s >= n_fa))
        def _():
            a = jnp.concatenate([act_ref[f, 0:m, :] for f in range(n_fa)], axis=1)
            out = (_dot(a, w2_ref[...].astype(BF16)) + b2_ref[...]).astype(BF16)
            stage_ref[s - n_fa, 0:m, :] = out

        @pl.when((nv == v) & (s == n_fa + n_jb - 1))
        def _():
            full = jnp.concatenate([stage_ref[j, 0:m, :] for j in range(n_jb)], axis=1)
            y_ref[0:m] = full.reshape(m, nt, LANES)


def _experts(item_e, item_xb, item_valid, xs, w1, b1, w2, b2):
    n_items = item_e.shape[0]
    n_exp, d, ff2 = w1.shape
    ff = ff2 // 2
    nt = d // LANES
    rb = ROW_BLOCK
    n_fa = ff // FF_CHUNK
    n_jb = d // OUT_CHUNK

    def fa(i, s, v):
        return jnp.where(v[i] > 0, jnp.minimum(s, n_fa - 1), n_fa - 1)

    def jb(i, s, v):
        return jnp.where(v[i] > 0, jnp.maximum(s - n_fa, 0), n_jb - 1)

    return pl.pallas_call(
        functools.partial(_experts_body, n_fa=n_fa, n_jb=n_jb),
        grid_spec=pltpu.PrefetchScalarGridSpec(
            num_scalar_prefetch=3,
            grid=(n_items, n_fa + n_jb),
            in_specs=[
                pl.BlockSpec((rb, nt, LANES), lambda i, s, e, xb, v: (xb[i], 0, 0)),
                pl.BlockSpec((None, d, FF_CHUNK), lambda i, s, e, xb, v: (e[i], 0, fa(i, s, v))),
                pl.BlockSpec((None, d, FF_CHUNK), lambda i, s, e, xb, v: (e[i], 0, n_fa + fa(i, s, v))),
                pl.BlockSpec((None, 1, FF_CHUNK), lambda i, s, e, xb, v: (e[i], 0, fa(i, s, v))),
                pl.BlockSpec((None, 1, FF_CHUNK), lambda i, s, e, xb, v: (e[i], 0, n_fa + fa(i, s, v))),
                pl.BlockSpec((None, ff, OUT_CHUNK), lambda i, s, e, xb, v: (e[i], 0, jb(i, s, v))),
                pl.BlockSpec((None, 1, OUT_CHUNK), lambda i, s, e, xb, v: (e[i], 0, jb(i, s, v))),
            ],
            out_specs=pl.BlockSpec((rb, nt, LANES), lambda i, s, e, xb, v: (i, 0, 0)),
            scratch_shapes=[pltpu.VMEM((n_fa, rb, FF_CHUNK), BF16), pltpu.VMEM((n_jb, rb, OUT_CHUNK), BF16)],
        ),
        out_shape=jax.ShapeDtypeStruct((n_items * rb, nt, LANES), BF16),
        compiler_params=_cparams(("arbitrary", "arbitrary")),
        name="experts",
    )(item_e, item_xb, item_valid, xs, w1, w1, b1.reshape(n_exp, 1, ff2), b1.reshape(n_exp, 1, ff2),
      w2, b2.reshape(n_exp, 1, d))


def _combine_body(pos_ref, route_ref, x2_ref, mod_ref, g_ref, y_hbm, o_ref, buf_ref, sem, *, tb, d):
    i = pl.program_id(0)
    n_steps = pl.num_programs(0)
    n = tb * TOP_K

    def issue_block(blk, slot):
        def issue(r, carry):
            for k in range(TOP_K):
                src = y_hbm.at[pos_ref[blk * n + r * TOP_K + k]]
                pltpu.make_async_copy(src, buf_ref.at[slot, k, r], sem.at[slot]).start()
            return carry

        lax.fori_loop(0, tb, issue, 0, unroll=4)

    @pl.when(i == 0)
    def _():
        issue_block(0, 0)

    @pl.when(i + 1 < n_steps)
    def _():
        issue_block(i + 1, (i + 1) % 2)

    slot = i % 2
    for k in range(TOP_K):
        pltpu.make_async_copy(y_hbm.at[pl.ds(0, tb)], buf_ref.at[slot, k], sem.at[slot]).wait()

    route = route_ref[...]
    moe = jnp.zeros((tb, d), F32)
    for k in range(TOP_K):
        p_k = lax.bitcast_convert_type(route[:, 2 * TOP_K + k:2 * TOP_K + k + 1], F32)
        moe = moe + p_k * buf_ref[slot, k].reshape(tb, d).astype(F32)
    x = x2_ref[...] + mod_ref[:, 5 * d:6 * d] * moe
    ms = jnp.mean(x * x, axis=-1, keepdims=True)
    o_ref[...] = x * lax.rsqrt(ms + NORM_EPS) * g_ref[...]


def _combine(pos_flat, route, x2, mod3, mod_base, tiles_per_mod, final_g, y_sorted, tb):
    m, d = x2.shape
    return pl.pallas_call(
        functools.partial(_combine_body, tb=tb, d=d),
        grid_spec=pltpu.PrefetchScalarGridSpec(
            num_scalar_prefetch=1,
            grid=(m // tb,),
            in_specs=[
                pl.BlockSpec((tb, LANES), lambda i, pos: (i, 0)),
                pl.BlockSpec((tb, d), lambda i, pos: (i, 0)),
                pl.BlockSpec((None, 1, mod3.shape[2]), lambda i, pos: (mod_base + i // tiles_per_mod, 0, 0)),
                pl.BlockSpec((1, d), lambda i, pos: (0, 0)),
                pl.BlockSpec(memory_space=pl.ANY),
            ],
            out_specs=pl.BlockSpec((tb, d), lambda i, pos: (i, 0)),
            scratch_shapes=[pltpu.VMEM((2, TOP_K, tb, d // LANES, LANES), BF16), pltpu.SemaphoreType.DMA((2,))],
        ),
        out_shape=jax.ShapeDtypeStruct((m, d), F32),
        compiler_params=_cparams(("arbitrary",)),
        name="combine",
    )(pos_flat, route, x2, mod3, final_g.reshape(1, d), y_sorted)


def _is_pow2(v):
    return v > 0 and (v & (v - 1)) == 0


def kernel(x_prompt, x_sample, state_gla_fwd, state_gla_bwd, c, c_ctx, norm1_g, w_ada, b_ada, w_in, w_gk_up, b_gk,
           gla_norm_g, w_gla_out, conv_w, w_conv_out, w_o, norm2_g, w_router, b_router, w_exp1, b_exp1, w_exp2,
           b_exp2, final_norm_g):
    batch, seq, d = x_prompt.shape
    dec_batch, dec_seq, _ = x_sample.shape
    depth = norm1_g.shape[0]
    assert depth == 1, "single trunk layer"
    h = GLA_HEADS
    dk_all = w_gk_up.shape[-1]
    dk, dv = dk_all // h, d // h
    n_exp = w_router.shape[-1]
    grid_w = 64
    assert _is_pow2(seq) and _is_pow2(grid_w) and dec_seq % grid_w == 0
    tm = 256
    tm_mix = 512
    assert tm % seq == 0 and tm % grid_w == 0, "conv segments must not straddle token tiles"
    assert tm_mix % tm == 0 and seq * batch % tm_mix == 0 and dec_seq % tm_mix == 0

    c_gk = 2 * dk_all + 2 * d
    w_in0 = w_in[0]
    w_att = w_in0[:, :c_gk].astype(BF16)
    n_rest = w_in0.shape[1] - c_gk - 2 * GK_RANK
    w_rest = _colcast(w_in0, c_gk + 2 * GK_RANK, n_rest, 1024 if n_rest % 1024 == 0 else 512)
    w_gk = w_in0[:, c_gk:c_gk + 2 * GK_RANK]
    zr = jnp.zeros((GK_RANK, dk_all), F32)
    w_up2 = jnp.stack([jnp.concatenate([w_gk_up[0, 0], zr], axis=0), jnp.concatenate([zr, w_gk_up[0, 1]], axis=0)])
    w_gla_out_b = w_gla_out[0].astype(BF16)
    w_conv_out_b = w_conv_out[0].astype(BF16)
    w_o_b = w_o[0].astype(BF16)

    n_cond = 1 + dec_batch
    cond = jnp.concatenate([c_ctx[None, :], c, jnp.zeros((8 - n_cond % 8, d), F32)], axis=0)
    mod = _ada(cond, w_ada[0], b_ada[0])
    mod3 = mod.reshape(mod.shape[0], 1, mod.shape[1])


    passes = []
    counts = jnp.zeros((1, n_exp), F32)
    for name, x3, n_seq, t, mod_base, seg in (("ctx", x_prompt, batch, seq, 0, seq),
                                              ("lat", x_sample, dec_batch, dec_seq, 1, grid_w)):
        m = n_seq * t
        x = x3.reshape(m, d)
        tiles_per_mod = (m // tm) if name == "ctx" else (t // tm)
        hb, gk_lo = _pre(x, mod3, mod_base, tiles_per_mod, norm1_g[0], w_gk, tm)
        z_att = _matmul(hb, w_att, min(m, 2048), 1024)
        z_rest = _matmul(hb, w_rest, min(m, 2048), 1024)
        if name == "ctx":
            yg, s_f, s_b = _gla(z_att, gk_lo, w_up2, b_gk[0], gla_norm_g[0], n_seq, t, dk, dv, emit_state=True)
        else:
            (yg,) = _gla(z_att, gk_lo, w_up2, b_gk[0], gla_norm_g[0], n_seq, t, dk, dv,
                         s0f=state_gla_fwd[:, 0], s0b=state_gla_bwd[:, 0])
            s_f = s_b = None
        merged = _mix(yg, z_rest, conv_w[0], w_gla_out_b, w_conv_out_b, tm_mix, seg, d, 0)
        x2, hp, route, counts = _route(merged, x, mod3, mod_base, tiles_per_mod * tm // tm_mix, w_o_b, norm2_g[0],
                                       w_router[0], b_router[0], counts, tm_mix)
        passes.append(dict(x2=x2, hp=hp, route=route, s_f=s_f, s_b=s_b, mod_base=mod_base,
                           tiles_per_mod=tiles_per_mod, shape=x3.shape))

    n_tok = sum(p["x2"].shape[0] for p in passes)
    n_items = (n_tok * TOP_K) // ROW_BLOCK + n_exp
    cnt = counts[0].astype(I32)
    nblk = (cnt + ROW_BLOCK - 1) // ROW_BLOCK
    blk_end = jnp.cumsum(nblk)
    blk0 = blk_end - nblk
    total = blk_end[-1]
    item = jnp.arange(n_items, dtype=I32)
    owner = lambda blk: jnp.sum((blk_end[None, :] <= blk[:, None]).astype(I32), axis=1)
    live = item < total
    item_e = jnp.where(live, jnp.minimum(owner(item), n_exp - 1), owner(total[None] - 1)[0])
    item_xb = jnp.where(live, item, total - 1)
    item_valid = jnp.where(live, jnp.clip(cnt[item_e] - (item - blk0[item_e]) * ROW_BLOCK, 0, ROW_BLOCK), 0)
    slot0 = blk0 * ROW_BLOCK

    for p in passes:
        e_idx = p["route"][:, 0:TOP_K]
        rank = p["route"][:, TOP_K:2 * TOP_K]
        p["pos"] = (slot0[e_idx] + rank).reshape(-1)
    sub_per_blk = ROW_BLOCK // SUB_TILE
    gran = jnp.arange(n_items * sub_per_blk, dtype=I32)
    covered = (gran % sub_per_blk + 1) * SUB_TILE <= item_valid[gran // sub_per_blk]
    zero_flags = jnp.where(covered, 0, 1).astype(I32)
    pos_all = jnp.concatenate([p["pos"] for p in passes])
    xs = _dispatch(pos_all, zero_flags, passes[0]["hp"], passes[1]["hp"], n_items * ROW_BLOCK, tm_mix)

    y_sorted = _experts(item_e, item_xb, item_valid, xs, w_exp1[0], b_exp1[0], w_exp2[0], b_exp2[0])

    outs = []
    for p in passes:
        y = _combine(p["pos"], p["route"], p["x2"], mod3, p["mod_base"], p["tiles_per_mod"] * tm // tm_mix,
                     final_norm_g, y_sorted, tm_mix)
        outs.append(y.reshape(p["shape"]))

    st_shape = (batch, depth, h, dk, dv)
    return (outs[0], outs[1], passes[0]["s_f"].reshape(st_shape), passes[0]["s_b"].reshape(st_shape))
```

```python
import functools

import jax
import jax.numpy as jnp
from jax import lax
from jax.experimental import pallas as pl
from jax.experimental.pallas import tpu as pltpu

F32 = jnp.float32
BF16 = jnp.bfloat16
I32 = jnp.int32

GLA_HEADS = 4
GK_RANK = 16
GATE_NORMALIZER = 16.0
TOP_K = 4
SWIGLU_ALPHA = 1.702
SWIGLU_LIMIT = 7.0
NORM_EPS = 1e-6

LANES = 128
BF16_ROWS = 16
VMEM_LIMIT = 56 * 1024 * 1024

GLA_CHUNK = 64
GLA_SUB = BF16_ROWS
GLA_UNROLL = 4
NEG_BIG = -1e30

ROW_BLOCK = 1024
SUB_TILE = 256
FF_CHUNK = 512
OUT_CHUNK = 512


def _cparams(sem):
    return pltpu.CompilerParams(dimension_semantics=sem, vmem_limit_bytes=VMEM_LIMIT)


def _dot(a, b):
    return jnp.dot(a, b, preferred_element_type=F32)


def _dot_nt(a, b):
    return lax.dot_general(a, b, (((1,), (1,)), ((), ())), preferred_element_type=F32)


def _dot_tn(a, b):
    return lax.dot_general(a, b, (((0,), (0,)), ((), ())), preferred_element_type=F32)


def _split_bf16(a, n):
    parts = []
    r = a
    for _ in range(n):
        p = r.astype(BF16)
        parts.append(p)
        r = r - p.astype(F32)
    return parts


def _dot_f32(a, b):
    a_hi, a_lo = _split_bf16(a, 2)
    b_hi, b_lo = _split_bf16(b, 2)
    return _dot(a_hi, b_hi) + (_dot(a_lo, b_hi) + _dot(a_hi, b_lo))


def _sigmoid(x):
    return 1.0 / (1.0 + jnp.exp(-x))


def _resident(shape):
    return pl.BlockSpec(shape, lambda i: (0,) * len(shape), pipeline_mode=pl.Buffered(1))


def _ada_body(c_ref, w_ref, b_ref, o_ref):
    c = c_ref[...]
    s = c * _sigmoid(c)
    o_ref[...] = _dot(s.astype(BF16), w_ref[...].astype(BF16)) + b_ref[...]


def _ada(cond, w_ada, b_ada):
    rows, d = cond.shape
    n = w_ada.shape[1]
    bn = 1024
    return pl.pallas_call(
        _ada_body,
        grid=(n // bn,),
        in_specs=[
            pl.BlockSpec((rows, d), lambda j: (0, 0)),
            pl.BlockSpec((d, bn), lambda j: (0, j)),
            pl.BlockSpec((1, bn), lambda j: (0, j)),
        ],
        out_specs=pl.BlockSpec((rows, bn), lambda j: (0, j)),
        out_shape=jax.ShapeDtypeStruct((rows, n), F32),
        compiler_params=_cparams(("arbitrary",)),
        name="ada",
    )(cond, w_ada, b_ada.reshape(1, n))


def _pre_body(x_ref, mod_ref, g_ref, wgk_ref, h_ref, gk_ref, *, d):
    x = x_ref[...]
    ms = jnp.mean(x * x, axis=-1, keepdims=True)
    y = x * lax.rsqrt(ms + NORM_EPS) * g_ref[...]
    h = y * (1.0 + mod_ref[:, d:2 * d]) + mod_ref[:, 0:d]
    h_ref[...] = h.astype(BF16)
    h_hi, h_lo = _split_bf16(h, 2)
    w_hi, w_lo = _split_bf16(wgk_ref[...], 2)
    gk_ref[...] = _dot_nt(h_hi, w_hi) + (_dot_nt(h_lo, w_hi) + _dot_nt(h_hi, w_lo))


def _pre(x, mod3, mod_base, tiles_per_mod, norm_g, w_t, gk_col0, r, tm):
    m, d = x.shape
    assert gk_col0 % r == 0 and r % 8 == 0
    return pl.pallas_call(
        functools.partial(_pre_body, d=d),
        grid=(m // tm,),
        in_specs=[
            pl.BlockSpec((tm, d), lambda i: (i, 0)),
            pl.BlockSpec((None, 1, mod3.shape[2]), lambda i: (mod_base + i // tiles_per_mod, 0, 0)),
            pl.BlockSpec((1, d), lambda i: (0, 0)),
            pl.BlockSpec((r, d), lambda i: (gk_col0 // r, 0)),
        ],
        out_specs=[
            pl.BlockSpec((tm, d), lambda i: (i, 0)),
            pl.BlockSpec((tm, r), lambda i: (i, 0)),
        ],
        out_shape=[jax.ShapeDtypeStruct((m, d), BF16), jax.ShapeDtypeStruct((m, r), F32)],
        compiler_params=_cparams(("arbitrary",)),
        name="pre",
    )(x, mod3, norm_g.reshape(1, d), w_t)


def _colcast_body(*refs, shift):
    o_ref = refs[-1]
    a = refs[0][...]
    if shift:
        a = jnp.concatenate([a[shift:, :], refs[1][:shift, :]], axis=0)
    o_ref[...] = a.T.astype(BF16)


def _colcast(w_t, col0, n_cols, bn):
    k = w_t.shape[1]
    base, shift = divmod(col0, bn)
    assert n_cols % bn == 0 and col0 + n_cols <= w_t.shape[0] and shift % 8 == 0
    in_specs = [pl.BlockSpec((bn, k), lambda j: (base + j, 0))]
    if shift:
        in_specs.append(pl.BlockSpec((bn, k), lambda j: (base + j + 1, 0)))
    return pl.pallas_call(
        functools.partial(_colcast_body, shift=shift),
        grid=(n_cols // bn,),
        in_specs=in_specs,
        out_specs=pl.BlockSpec((k, bn), lambda j: (0, j)),
        out_shape=jax.ShapeDtypeStruct((k, n_cols), BF16),
        compiler_params=_cparams(("arbitrary",)),
        name="colcast",
    )(*([w_t] * len(in_specs)))


def _mm_body(a_ref, b_ref, o_ref):
    o_ref[...] = _dot(a_ref[...], b_ref[...]).astype(o_ref.dtype)


def _matmul(a, b, bm, bn):
    m, k = a.shape
    n = b.shape[1]
    while n % bn:
        bn //= 2
    assert bn % LANES == 0
    return pl.pallas_call(
        _mm_body,
        grid=(m // bm, n // bn),
        in_specs=[
            pl.BlockSpec((bm, k), lambda i, j: (i, 0)),
            pl.BlockSpec((k, bn), lambda i, j: (0, j)),
        ],
        out_specs=pl.BlockSpec((bm, bn), lambda i, j: (i, j)),
        out_shape=jax.ShapeDtypeStruct((m, n), BF16),
        compiler_params=_cparams(("arbitrary", "arbitrary")),
        name="in_proj",
    )(a, b)


def _gla_prep(q, k, v, gk_lo, w_up, b_up, reverse):
    c, dk = q.shape
    gk = _dot_f32(gk_lo, w_up) + b_up
    g = (jnp.minimum(gk, 0.0) - jnp.log(1.0 + jnp.exp(-jnp.abs(gk)))) * (1.0 / GATE_NORMALIZER)
    row = lax.broadcasted_iota(I32, (c, c), 0)
    col = lax.broadcasted_iota(I32, (c, c), 1)
    tri = (col >= row) if reverse else (col <= row)
    tri_b = jnp.where(tri, 1.0, 0.0).astype(BF16)
    g1, g2 = _split_bf16(g, 2)
    b = _dot(tri_b, g1) + _dot(tri_b, g2)

    qf = q.astype(F32) * (dk ** -0.5)
    kf = k.astype(F32)
    q_in = (qf * jnp.exp(b)).astype(BF16)

    rowv = lax.broadcasted_iota(I32, (c, 1), 0)
    lhs, rhs = [], []
    for j in range(c // GLA_SUB):
        lo, hi = j * GLA_SUB, (j + 1) * GLA_SUB
        e_row = lo if reverse else hi - 1
        e = b[e_row:e_row + 1, :]
        in_blk = (rowv >= lo) & (rowv < hi)
        active = (rowv < hi) if reverse else (rowv >= lo)
        lhs.append((qf * jnp.exp(jnp.where(active, b - e, NEG_BIG))).astype(BF16))
        rhs.append((kf * jnp.exp(jnp.where(in_blk, e - b, NEG_BIG))).astype(BF16))
    p = _dot_nt(jnp.concatenate(lhs, axis=1), jnp.concatenate(rhs, axis=1))
    p = jnp.where(tri, p, 0.0)
    o_intra = _dot(p.astype(BF16), v)

    l_row = 0 if reverse else c - 1
    bl = b[l_row:l_row + 1, :]
    ks = (kf * jnp.exp(bl - b)).astype(BF16)
    return q_in, ks, jnp.exp(bl), o_intra


def _gla_state_step(prep, v, st_ref):
    q_in, ks, decay, o_intra = prep
    st = st_ref[...]
    o = o_intra + _dot_nt(q_in, st.astype(BF16))
    st_ref[...] = st * decay + _dot_tn(v, ks)
    return o


def _gla_body(*refs, t, unroll, has_init, emit_state):
    q_ref, k_ref, v_ref, og_ref, gk_ref, wup_ref, bup_ref, gn_ref = refs[:8]
    pos = 8
    if has_init:
        s0f_ref, s0b_ref = refs[pos:pos + 2]
        pos += 2
    y_ref = refs[pos]
    pos += 1
    if emit_state:
        sf_ref, sb_ref = refs[pos:pos + 2]
        pos += 2
    o_ref, stf_ref, stb_ref = refs[pos:pos + 3]

    c = GLA_CHUNK
    n = t // c
    if has_init:
        stf_ref[...] = s0f_ref[...].T
        stb_ref[...] = s0b_ref[...].T
    else:
        stf_ref[...] = jnp.zeros_like(stf_ref)
        stb_ref[...] = jnp.zeros_like(stb_ref)

    wup_f, wup_b = wup_ref[0], wup_ref[1]
    bup_f, bup_b = bup_ref[0], bup_ref[1]

    def rows(r):
        return pl.ds(pl.multiple_of(r * c, c), c)

    def make_step(first_touch):
        def step(i, carry):
            jobs = []
            for u in range(unroll):
                cf = i * unroll + u
                jobs.append((rows(cf), wup_f, bup_f, stf_ref, False))
                jobs.append((rows(n - 1 - cf), wup_b, bup_b, stb_ref, True))
            preps = [_gla_prep(q_ref[r, :], k_ref[r, :], v_ref[r, :], gk_ref[r, :], w, bias, rev)
                     for r, w, bias, _, rev in jobs]
            for (r, _, _, st_ref, _), prep in zip(jobs, preps):
                o = _gla_state_step(prep, v_ref[r, :], st_ref)
                if first_touch:
                    o_ref[r, :] = o
                else:
                    o_ref[r, :] += o
            return carry
        return step

    half = n // (2 * unroll)
    lax.fori_loop(0, half, make_step(True), 0)
    lax.fori_loop(half, 2 * half, make_step(False), 0)

    if emit_state:
        sf_ref[...] = stf_ref[...].T
        sb_ref[...] = stb_ref[...].T

    blk = min(t, 256)

    def fin(i, carry):
        r = pl.multiple_of(i * blk, blk)
        o = o_ref[pl.ds(r, blk), :]
        ms = jnp.mean(o * o, axis=-1, keepdims=True)
        og = og_ref[pl.ds(r, blk), :].astype(F32)
        y = o * lax.rsqrt(ms + NORM_EPS) * gn_ref[...] * (og * _sigmoid(og))
        y_ref[pl.ds(r, blk), :] = y.astype(BF16)
        return carry

    lax.fori_loop(0, t // blk, fin, 0)


def _gla(z, gk_lo, w_gk_up, b_gk, gn, n_seq, t, dk, dv, s0f=None, s0b=None, emit_state=False):
    h = GLA_HEADS
    m = n_seq * t
    n_chunks = t // GLA_CHUNK
    unroll = max(1, min(GLA_UNROLL, n_chunks // 2))
    assert t % (2 * unroll * GLA_CHUNK) == 0
    has_init = s0f is not None
    qk_blocks = (h * dk) // dk
    in_specs = [
        pl.BlockSpec((t, dk), lambda s, hh: (s, hh)),
        pl.BlockSpec((t, dk), lambda s, hh: (s, qk_blocks + hh)),
        pl.BlockSpec((t, dv), lambda s, hh: (s, (2 * h * dk) // dv + hh)),
        pl.BlockSpec((t, dv), lambda s, hh: (s, (2 * h * dk) // dv + h + hh)),
        pl.BlockSpec((t, 2 * GK_RANK), lambda s, hh: (s, 0)),
        pl.BlockSpec((2, 2 * GK_RANK, dk), lambda s, hh: (0, 0, hh)),
        pl.BlockSpec((2, 1, dk), lambda s, hh: (0, 0, hh)),
        pl.BlockSpec((1, dv), lambda s, hh: (0, 0)),
    ]
    args = [z, z, z, z, gk_lo, w_gk_up, b_gk.reshape(2, 1, h * dk), gn.reshape(1, dv)]
    if has_init:
        in_specs += [pl.BlockSpec((None, dk, dv), lambda s, hh: (s * h + hh, 0, 0))] * 2
        args += [s0f.reshape(n_seq * h, dk, dv), s0b.reshape(n_seq * h, dk, dv)]
    out_specs = [pl.BlockSpec((t, dv), lambda s, hh: (s, hh))]
    out_shape = [jax.ShapeDtypeStruct((m, h * dv), BF16)]
    if emit_state:
        out_specs += [pl.BlockSpec((None, dk, dv), lambda s, hh: (s * h + hh, 0, 0))] * 2
        out_shape += [jax.ShapeDtypeStruct((n_seq * h, dk, dv), F32)] * 2
    return pl.pallas_call(
        functools.partial(_gla_body, t=t, unroll=unroll, has_init=has_init, emit_state=emit_state),
        grid=(n_seq, h),
        in_specs=in_specs,
        out_specs=out_specs,
        out_shape=out_shape,
        scratch_shapes=[pltpu.VMEM((t, dv), F32), pltpu.VMEM((dv, dk), F32), pltpu.VMEM((dv, dk), F32)],
        compiler_params=_cparams(("arbitrary", "arbitrary")),
        name="gla",
    )(*args)


def _mix_body(yg_ref, cx_ref, cb_ref, cc_ref, ga_ref, gb_ref, cw_ref, wg_ref, wc_ref, o_ref, *, seg):
    tm = yg_ref.shape[0]
    s = cc_ref[...].astype(F32) * cx_ref[...].astype(F32)
    rowv = lax.broadcasted_iota(I32, (tm, 1), 0)
    pos = rowv & (seg - 1)
    prev = jnp.where(pos == 0, 0.0, pltpu.roll(s, 1, axis=0))
    nxt = jnp.where(pos == seg - 1, 0.0, pltpu.roll(s, tm - 1, axis=0))
    conv = prev * cw_ref[0:1, :] + s * cw_ref[1:2, :] + nxt * cw_ref[2:3, :]
    u = (cb_ref[...].astype(F32) * conv).astype(BF16)
    y_gla = _dot(yg_ref[...], wg_ref[...])
    y_conv = _dot(u, wc_ref[...])
    merged = _sigmoid(ga_ref[...].astype(F32)) * y_gla + _sigmoid(gb_ref[...].astype(F32)) * y_conv
    o_ref[...] = merged.astype(BF16)


def _mix(yg, z, conv_w, w_gla_out, w_conv_out, tm, seg, d, col0):
    m = yg.shape[0]
    zspec = lambda c: pl.BlockSpec((tm, d), lambda i: (i, col0 + c))
    const = lambda shape: pl.BlockSpec(shape, lambda i: (0, 0))
    return pl.pallas_call(
        functools.partial(_mix_body, seg=seg),
        grid=(m // tm,),
        in_specs=[pl.BlockSpec((tm, d), lambda i: (i, 0)), zspec(0), zspec(1), zspec(2), zspec(3), zspec(4),
                  const(conv_w.shape), _resident(w_gla_out.shape), _resident(w_conv_out.shape)],
        out_specs=pl.BlockSpec((tm, d), lambda i: (i, 0)),
        out_shape=jax.ShapeDtypeStruct((m, d), BF16),
        compiler_params=_cparams(("arbitrary",)),
        name="mix",
    )(yg, z, z, z, z, z, conv_w, w_gla_out, w_conv_out)


def _route_body(mg_ref, x_ref, mod_ref, wo_ref, g_ref, wr_ref, br_ref, cnt_in_ref,
                x2_ref, hp_ref, route_ref, prob_ref, cnt_ref, carry_ref, *, d, n_exp):
    i = pl.program_id(0)
    tm = x_ref.shape[0]

    @pl.when(i == 0)
    def _():
        carry_ref[...] = cnt_in_ref[...]

    x2 = x_ref[...] + mod_ref[:, 2 * d:3 * d] * _dot(mg_ref[...], wo_ref[...])
    x2_ref[...] = x2
    ms = jnp.mean(x2 * x2, axis=-1, keepdims=True)
    h = x2 * lax.rsqrt(ms + NORM_EPS) * g_ref[...] * (1.0 + mod_ref[:, 4 * d:5 * d]) + mod_ref[:, 3 * d:4 * d]
    hp_ref[...] = h.astype(BF16).reshape(tm, d // LANES, LANES)

    logits = _dot_f32(h, wr_ref[...]) + br_ref[...]
    lane = lax.broadcasted_iota(I32, (tm, n_exp), 1).astype(F32)
    work = logits
    vals, hots = [], []
    for _ in range(TOP_K):
        mx = jnp.max(work, axis=-1, keepdims=True)
        idx = jnp.min(jnp.where(work == mx, lane, float(n_exp)), axis=-1, keepdims=True)
        hot = lane == idx
        vals.append(mx)
        hots.append(hot)
        work = jnp.where(hot, -jnp.inf, work)
    exps = [jnp.exp(v - vals[0]) for v in vals]
    inv = 1.0 / (exps[0] + exps[1] + exps[2] + exps[3])

    cnt = jnp.zeros((tm, n_exp), F32)
    for hot in hots:
        cnt = cnt + jnp.where(hot, 1.0, 0.0)
    r = lax.broadcasted_iota(I32, (tm, tm), 0)
    cc = lax.broadcasted_iota(I32, (tm, tm), 1)
    strict = jnp.where(cc < r, 1.0, 0.0).astype(BF16)
    before = _dot(strict, cnt.astype(BF16)) + carry_ref[...]
    carry_ref[...] = carry_ref[...] + jnp.sum(cnt, axis=0, keepdims=True)

    out_lane = lax.broadcasted_iota(I32, (tm, LANES), 1)
    route = jnp.zeros((tm, LANES), I32)
    probs = jnp.zeros((tm, LANES), F32)
    for k in range(TOP_K):
        e_k = jnp.sum(jnp.where(hots[k], lane, 0.0), axis=-1, keepdims=True).astype(I32)
        rank_k = jnp.sum(jnp.where(hots[k], before, 0.0), axis=-1, keepdims=True).astype(I32)
        route = jnp.where(out_lane == k, e_k, route)
        route = jnp.where(out_lane == TOP_K + k, rank_k, route)
        probs = jnp.where(out_lane == k, exps[k] * inv, probs)
    route_ref[...] = route
    prob_ref[...] = probs

    @pl.when(i == pl.num_programs(0) - 1)
    def _():
        cnt_ref[...] = carry_ref[...]


def _route(merged, x, mod3, mod_base, tiles_per_mod, w_o, norm_g, w_router, b_router, counts_in, tm):
    m, d = x.shape
    n_exp = w_router.shape[1]
    const = lambda shape: pl.BlockSpec(shape, lambda i: (0, 0))
    tile = lambda w: pl.BlockSpec((tm, w), lambda i: (i, 0))
    return pl.pallas_call(
        functools.partial(_route_body, d=d, n_exp=n_exp),
        grid=(m // tm,),
        in_specs=[tile(d), tile(d),
                  pl.BlockSpec((None, 1, mod3.shape[2]), lambda i: (mod_base + i // tiles_per_mod, 0, 0)),
                  _resident(w_o.shape), const((1, d)), const(w_router.shape), const((1, n_exp)),
                  const((1, n_exp))],
        out_specs=[tile(d), pl.BlockSpec((tm, d // LANES, LANES), lambda i: (i, 0, 0)), tile(LANES), tile(LANES),
                   const((1, n_exp))],
        out_shape=[jax.ShapeDtypeStruct((m, d), F32), jax.ShapeDtypeStruct((m, d // LANES, LANES), BF16),
                   jax.ShapeDtypeStruct((m, LANES), I32), jax.ShapeDtypeStruct((m, LANES), F32),
                   jax.ShapeDtypeStruct((1, n_exp), F32)],
        scratch_shapes=[pltpu.VMEM((1, n_exp), F32)],
        compiler_params=_cparams(("arbitrary",)),
        name="route",
    )(merged, x, mod3, w_o, norm_g.reshape(1, d), w_router, b_router.reshape(1, n_exp), counts_in)


def _dispatch_body(pos_ref, zero_ref, hp_a_ref, hp_b_ref, xs_ref, sem, zbuf_ref, *, tb, n_a):
    i = pl.program_id(0)
    n = tb * TOP_K

    @pl.when(i == 0)
    def _():
        zbuf_ref[...] = jnp.zeros_like(zbuf_ref)
        n_gran = zero_ref.shape[0]

        def fill(g, carry):
            @pl.when(zero_ref[g] != 0)
            def _():
                pltpu.make_async_copy(zbuf_ref, xs_ref.at[pl.ds(g * SUB_TILE, SUB_TILE)], sem).start()
            return carry

        def drain(g, carry):
            @pl.when(zero_ref[g] != 0)
            def _():
                pltpu.make_async_copy(zbuf_ref, xs_ref.at[pl.ds(0, SUB_TILE)], sem).wait()
            return carry

        lax.fori_loop(0, n_gran, fill, 0)
        lax.fori_loop(0, n_gran, drain, 0)

    def scatter(hp_ref):
        def issue(r, carry):
            for k in range(TOP_K):
                slot = pos_ref[i * n + r * TOP_K + k]
                pltpu.make_async_copy(hp_ref.at[r], xs_ref.at[slot], sem).start()
            return carry

        lax.fori_loop(0, tb, issue, 0, unroll=4)
        for k in range(TOP_K):
            pltpu.make_async_copy(hp_ref, xs_ref.at[pl.ds(0, tb)], sem).wait()

    @pl.when(i < n_a)
    def _():
        scatter(hp_a_ref)

    @pl.when(i >= n_a)
    def _():
        scatter(hp_b_ref)


def _dispatch(pos_flat, zero_flags, hp_a, hp_b, n_slots, tb):
    nt = hp_a.shape[1]
    n_a, n_b = hp_a.shape[0] // tb, hp_b.shape[0] // tb
    return pl.pallas_call(
        functools.partial(_dispatch_body, tb=tb, n_a=n_a),
        grid_spec=pltpu.PrefetchScalarGridSpec(
            num_scalar_prefetch=2,
            grid=(n_a + n_b,),
            in_specs=[
                pl.BlockSpec((tb, nt, LANES), lambda i, pos, zf: (jnp.minimum(i, n_a - 1), 0, 0)),
                pl.BlockSpec((tb, nt, LANES), lambda i, pos, zf: (jnp.maximum(i - n_a, 0), 0, 0)),
            ],
            out_specs=pl.BlockSpec(memory_space=pl.ANY),
            scratch_shapes=[pltpu.SemaphoreType.DMA(()), pltpu.VMEM((SUB_TILE, nt, LANES), BF16)],
        ),
        out_shape=jax.ShapeDtypeStruct((n_slots, nt, LANES), BF16),
        compiler_params=_cparams(("arbitrary",)),
        name="dispatch",
    )(pos_flat, zero_flags, hp_a, hp_b)


def _experts_body(e_ref, xb_ref, valid_ref, x_ref, w1g_ref, w1u_ref, b1g_ref, b1u_ref, w2_ref, b2_ref,
                  y_ref, act_ref, stage_ref, *, n_fa, n_jb):
    del e_ref, xb_ref
    i = pl.program_id(0)
    s = pl.program_id(1)
    valid = valid_ref[i]
    _, rb, oc = stage_ref.shape
    d = n_jb * oc
    nt = d // LANES
    n_sub = rb // SUB_TILE
    nv = (valid + SUB_TILE - 1) // SUB_TILE

    @pl.when(s == 0)
    def _():
        for t in range(n_sub):
            rows = slice(t * SUB_TILE, (t + 1) * SUB_TILE)

            @pl.when(t < nv)
            def _():
                x2d = x_ref[rows].reshape(SUB_TILE, d)
                for j in range(n_jb):
                    stage_ref[j, rows, :] = x2d[:, j * oc:(j + 1) * oc]

            @pl.when(t >= nv)
            def _():
                y_ref[rows] = jnp.zeros((SUB_TILE, nt, LANES), BF16)

    for v in range(1, n_sub + 1):
        m = v * SUB_TILE

        @pl.when((nv == v) & (s < n_fa))
        def _():
            xs = jnp.concatenate([stage_ref[j, 0:m, :] for j in range(n_jb)], axis=1)
            gate = jnp.minimum(_dot(xs, w1g_ref[...].astype(BF16)) + b1g_ref[...], SWIGLU_LIMIT)
            up = jnp.clip(_dot(xs, w1u_ref[...].astype(BF16)) + b1u_ref[...], -SWIGLU_LIMIT, SWIGLU_LIMIT)
            act = (up + 1.0) * gate * _sigmoid(SWIGLU_ALPHA * gate)
            act_ref[s, 0:m, :] = act.astype(BF16)

        @pl.when((nv == v) & (s >= n_fa))
        def _():
            a = jnp.concatenate([act_ref[f, 0:m, :] for f in range(n_fa)], axis=1)
            out = (_dot(a, w2_ref[...].astype(BF16)) + b2_ref[...]).astype(BF16)
            stage_ref[s - n_fa, 0:m, :] = out

        @pl.when((nv == v) & (s == n_fa + n_jb - 1))
        def _():
            full = jnp.concatenate([stage_ref[j, 0:m, :] for j in range(n_jb)], axis=1)
            y_ref[0:m] = full.reshape(m, nt, LANES)


def _experts(item_e, item_xb, item_valid, xs, w1, b1, w2, b2):
    n_items = item_e.shape[0]
    n_exp, d, ff2 = w1.shape
    ff = ff2 // 2
    nt = d // LANES
    rb = ROW_BLOCK
    n_fa = ff // FF_CHUNK
    n_jb = d // OUT_CHUNK

    def fa(i, s, v):
        return jnp.where(v[i] > 0, jnp.minimum(s, n_fa - 1), n_fa - 1)

    def jb(i, s, v):
        return jnp.where(v[i] > 0, jnp.maximum(s - n_fa, 0), n_jb - 1)

    return pl.pallas_call(
        functools.partial(_experts_body, n_fa=n_fa, n_jb=n_jb),
        grid_spec=pltpu.PrefetchScalarGridSpec(
            num_scalar_prefetch=3,
            grid=(n_items, n_fa + n_jb),
            in_specs=[
                pl.BlockSpec((rb, nt, LANES), lambda i, s, e, xb, v: (xb[i], 0, 0)),
                pl.BlockSpec((None, d, FF_CHUNK), lambda i, s, e, xb, v: (e[i], 0, fa(i, s, v))),
                pl.BlockSpec((None, d, FF_CHUNK), lambda i, s, e, xb, v: (e[i], 0, n_fa + fa(i, s, v))),
                pl.BlockSpec((None, 1, FF_CHUNK), lambda i, s, e, xb, v: (e[i], 0, fa(i, s, v))),
                pl.BlockSpec((None, 1, FF_CHUNK), lambda i, s, e, xb, v: (e[i], 0, n_fa + fa(i, s, v))),
                pl.BlockSpec((None, ff, OUT_CHUNK), lambda i, s, e, xb, v: (e[i], 0, jb(i, s, v))),
                pl.BlockSpec((None, 1, OUT_CHUNK), lambda i, s, e, xb, v: (e[i], 0, jb(i, s, v))),
            ],
            out_specs=pl.BlockSpec((rb, nt, LANES), lambda i, s, e, xb, v: (i, 0, 0)),
            scratch_shapes=[pltpu.VMEM((n_fa, rb, FF_CHUNK), BF16), pltpu.VMEM((n_jb, rb, OUT_CHUNK), BF16)],
        ),
        out_shape=jax.ShapeDtypeStruct((n_items * rb, nt, LANES), BF16),
        compiler_params=_cparams(("arbitrary", "arbitrary")),
        name="experts",
    )(item_e, item_xb, item_valid, xs, w1, w1, b1.reshape(n_exp, 1, ff2), b1.reshape(n_exp, 1, ff2),
      w2, b2.reshape(n_exp, 1, d))


def _combine_body(pos_ref, prob_ref, x2_ref, mod_ref, g_ref, y_hbm, o_ref, buf_ref, sem, *, tb, d):
    i = pl.program_id(0)
    n_steps = pl.num_programs(0)
    n = tb * TOP_K

    def issue_block(blk, slot):
        def issue(r, carry):
            for k in range(TOP_K):
                src = y_hbm.at[pos_ref[blk * n + r * TOP_K + k]]
                pltpu.make_async_copy(src, buf_ref.at[slot, k, r], sem.at[slot]).start()
            return carry

        lax.fori_loop(0, tb, issue, 0, unroll=4)

    @pl.when(i == 0)
    def _():
        issue_block(0, 0)

    @pl.when(i + 1 < n_steps)
    def _():
        issue_block(i + 1, (i + 1) % 2)

    slot = i % 2
    for k in range(TOP_K):
        pltpu.make_async_copy(y_hbm.at[pl.ds(0, tb)], buf_ref.at[slot, k], sem.at[slot]).wait()

    probs = prob_ref[...]
    moe = jnp.zeros((tb, d), F32)
    for k in range(TOP_K):
        moe = moe + probs[:, k:k + 1] * buf_ref[slot, k].reshape(tb, d).astype(F32)
    x = x2_ref[...] + mod_ref[:, 5 * d:6 * d] * moe
    ms = jnp.mean(x * x, axis=-1, keepdims=True)
    o_ref[...] = x * lax.rsqrt(ms + NORM_EPS) * g_ref[...]


def _combine(pos_flat, probs, x2, mod3, mod_base, tiles_per_mod, final_g, y_sorted, tb):
    m, d = x2.shape
    return pl.pallas_call(
        functools.partial(_combine_body, tb=tb, d=d),
        grid_spec=pltpu.PrefetchScalarGridSpec(
            num_scalar_prefetch=1,
            grid=(m // tb,),
            in_specs=[
                pl.BlockSpec((tb, LANES), lambda i, pos: (i, 0)),
                pl.BlockSpec((tb, d), lambda i, pos: (i, 0)),
                pl.BlockSpec((None, 1, mod3.shape[2]), lambda i, pos: (mod_base + i // tiles_per_mod, 0, 0)),
                pl.BlockSpec((1, d), lambda i, pos: (0, 0)),
                pl.BlockSpec(memory_space=pl.ANY),
            ],
            out_specs=pl.BlockSpec((tb, d), lambda i, pos: (i, 0)),
            scratch_shapes=[pltpu.VMEM((2, TOP_K, tb, d // LANES, LANES), BF16), pltpu.SemaphoreType.DMA((2,))],
        ),
        out_shape=jax.ShapeDtypeStruct((m, d), F32),
        compiler_params=_cparams(("arbitrary",)),
        name="combine",
    )(pos_flat, probs, x2, mod3, final_g.reshape(1, d), y_sorted)


def _is_pow2(v):
    return v > 0 and (v & (v - 1)) == 0


def kernel(x_prompt, x_sample, state_gla_fwd, state_gla_bwd, c, c_ctx, norm1_g, w_ada, b_ada, w_in, w_gk_up, b_gk,
           gla_norm_g, w_gla_out, conv_w, w_conv_out, w_o, norm2_g, w_router, b_router, w_exp1, b_exp1, w_exp2,
           b_exp2, final_norm_g):
    batch, seq, d = x_prompt.shape
    dec_batch, dec_seq, _ = x_sample.shape
    depth = norm1_g.shape[0]
    assert depth == 1, "single trunk layer"
    h = GLA_HEADS
    dk_all = w_gk_up.shape[-1]
    dk, dv = dk_all // h, d // h
    n_exp = w_router.shape[-1]
    grid_w = 64
    assert _is_pow2(seq) and _is_pow2(grid_w) and dec_seq % grid_w == 0
    tm = 256
    tm_mix = 512
    assert tm % seq == 0 and tm % grid_w == 0, "conv segments must not straddle token tiles"
    assert tm_mix % tm == 0 and seq * batch % tm_mix == 0 and dec_seq % tm_mix == 0

    c_gk = 2 * dk_all + 2 * d
    w_in_t = jnp.transpose(w_in[0])
    n_rest = w_in_t.shape[0] - c_gk - 2 * GK_RANK
    cast_bn = 512
    w_att = _colcast(w_in_t, 0, c_gk, cast_bn)
    w_rest = _colcast(w_in_t, c_gk + 2 * GK_RANK, n_rest, cast_bn)
    zr = jnp.zeros((GK_RANK, dk_all), F32)
    w_up2 = jnp.stack([jnp.concatenate([w_gk_up[0, 0], zr], axis=0), jnp.concatenate([zr, w_gk_up[0, 1]], axis=0)])
    w_gla_out_b = w_gla_out[0].astype(BF16)
    w_conv_out_b = w_conv_out[0].astype(BF16)
    w_o_b = w_o[0].astype(BF16)

    n_cond = 1 + dec_batch
    cond = jnp.concatenate([c_ctx[None, :], c, jnp.zeros((8 - n_cond % 8, d), F32)], axis=0)
    mod = _ada(cond, w_ada[0], b_ada[0])
    mod3 = mod.reshape(mod.shape[0], 1, mod.shape[1])


    passes = []
    counts = jnp.zeros((1, n_exp), F32)
    for name, x3, n_seq, t, mod_base, seg in (("ctx", x_prompt, batch, seq, 0, seq),
                                              ("lat", x_sample, dec_batch, dec_seq, 1, grid_w)):
        m = n_seq * t
        x = x3.reshape(m, d)
        tiles_per_mod = (m // tm) if name == "ctx" else (t // tm)
        hb, gk_lo = _pre(x, mod3, mod_base, tiles_per_mod, norm1_g[0], w_in_t, c_gk, 2 * GK_RANK, tm)
        z_att = _matmul(hb, w_att, min(m, 2048), 1024)
        z_rest = _matmul(hb, w_rest, min(m, 2048), 1024)
        if name == "ctx":
            yg, s_f, s_b = _gla(z_att, gk_lo, w_up2, b_gk[0], gla_norm_g[0], n_seq, t, dk, dv, emit_state=True)
        else:
            (yg,) = _gla(z_att, gk_lo, w_up2, b_gk[0], gla_norm_g[0], n_seq, t, dk, dv,
                         s0f=state_gla_fwd[:, 0], s0b=state_gla_bwd[:, 0])
            s_f = s_b = None
        merged = _mix(yg, z_rest, conv_w[0], w_gla_out_b, w_conv_out_b, tm_mix, seg, d, 0)
        x2, hp, route, probs, counts = _route(merged, x, mod3, mod_base, tiles_per_mod * tm // tm_mix, w_o_b,
                                              norm2_g[0], w_router[0], b_router[0], counts, tm_mix)
        passes.append(dict(x2=x2, hp=hp, route=route, probs=probs, s_f=s_f, s_b=s_b, mod_base=mod_base,
                           tiles_per_mod=tiles_per_mod, shape=x3.shape))

    n_tok = sum(p["x2"].shape[0] for p in passes)
    n_items = (n_tok * TOP_K) // ROW_BLOCK + n_exp
    cnt = counts[0].astype(I32)
    nblk = (cnt + ROW_BLOCK - 1) // ROW_BLOCK
    blk_end = jnp.cumsum(nblk)
    blk0 = blk_end - nblk
    total = blk_end[-1]
    item = jnp.arange(n_items, dtype=I32)
    owner = lambda blk: jnp.sum((blk_end[None, :] <= blk[:, None]).astype(I32), axis=1)
    live = item < total
    item_e = jnp.where(live, jnp.minimum(owner(item), n_exp - 1), owner(total[None] - 1)[0])
    item_xb = jnp.where(live, item, total - 1)
    item_valid = jnp.where(live, jnp.clip(cnt[item_e] - (item - blk0[item_e]) * ROW_BLOCK, 0, ROW_BLOCK), 0)
    slot0 = blk0 * ROW_BLOCK

    for p in passes:
        e_idx = p["route"][:, 0:TOP_K]
        rank = p["route"][:, TOP_K:2 * TOP_K]
        p["pos"] = (slot0[e_idx] + rank).reshape(-1)
    sub_per_blk = ROW_BLOCK // SUB_TILE
    gran = jnp.arange(n_items * sub_per_blk, dtype=I32)
    covered = (gran % sub_per_blk + 1) * SUB_TILE <= item_valid[gran // sub_per_blk]
    zero_flags = jnp.where(covered, 0, 1).astype(I32)
    pos_all = jnp.concatenate([p["pos"] for p in passes])
    xs = _dispatch(pos_all, zero_flags, passes[0]["hp"], passes[1]["hp"], n_items * ROW_BLOCK, tm_mix)

    y_sorted = _experts(item_e, item_xb, item_valid, xs, w_exp1[0], b_exp1[0], w_exp2[0], b_exp2[0])

    outs = []
    for p in passes:
        y = _combine(p["pos"], p["probs"], p["x2"], mod3, p["mod_base"], p["tiles_per_mod"], final_norm_g,
                     y_sorted, tm)
        outs.append(y.reshape(p["shape"]))

    st_shape = (batch, depth, h, dk, dv)
    return (outs[0], outs[1], passes[0]["s_f"].reshape(st_shape), passes[0]["s_b"].reshape(st_shape))
```

```python
import functools

import jax
import jax.numpy as jnp
from jax import lax
from jax.experimental import pallas as pl
from jax.experimental.pallas import tpu as pltpu

F32 = jnp.float32
BF16 = jnp.bfloat16
I32 = jnp.int32

GLA_HEADS = 4
GK_RANK = 16
GATE_NORMALIZER = 16.0
TOP_K = 4
SWIGLU_ALPHA = 1.702
SWIGLU_LIMIT = 7.0
NORM_EPS = 1e-6

LANES = 128
BF16_ROWS = 16
VMEM_LIMIT = 56 * 1024 * 1024

GLA_CHUNK = 64
GLA_SUB = BF16_ROWS
GLA_UNROLL = 4
NEG_BIG = -1e30

ROW_BLOCK = 1024
SUB_TILE = 256
FF_CHUNK = 512
OUT_CHUNK = 512


def _cparams(sem):
    return pltpu.CompilerParams(dimension_semantics=sem, vmem_limit_bytes=VMEM_LIMIT)


def _dot(a, b):
    return jnp.dot(a, b, preferred_element_type=F32)


def _dot_nt(a, b):
    return lax.dot_general(a, b, (((1,), (1,)), ((), ())), preferred_element_type=F32)


def _dot_tn(a, b):
    return lax.dot_general(a, b, (((0,), (0,)), ((), ())), preferred_element_type=F32)


def _split_bf16(a, n):
    parts = []
    r = a
    for _ in range(n):
        p = r.astype(BF16)
        parts.append(p)
        r = r - p.astype(F32)
    return parts


def _dot_f32(a, b):
    a_hi, a_lo = _split_bf16(a, 2)
    b_hi, b_lo = _split_bf16(b, 2)
    return _dot(a_hi, b_hi) + (_dot(a_lo, b_hi) + _dot(a_hi, b_lo))


def _sigmoid(x):
    return 1.0 / (1.0 + jnp.exp(-x))


def _resident(shape):
    return pl.BlockSpec(shape, lambda i: (0,) * len(shape), pipeline_mode=pl.Buffered(1))


def _ada_body(c_ref, w_ref, b_ref, o_ref):
    c = c_ref[...]
    s = c * _sigmoid(c)
    o_ref[...] = _dot(s.astype(BF16), w_ref[...].astype(BF16)) + b_ref[...]


def _ada(cond, w_ada, b_ada):
    rows, d = cond.shape
    n = w_ada.shape[1]
    bn = 1024
    return pl.pallas_call(
        _ada_body,
        grid=(n // bn,),
        in_specs=[
            pl.BlockSpec((rows, d), lambda j: (0, 0)),
            pl.BlockSpec((d, bn), lambda j: (0, j)),
            pl.BlockSpec((1, bn), lambda j: (0, j)),
        ],
        out_specs=pl.BlockSpec((rows, bn), lambda j: (0, j)),
        out_shape=jax.ShapeDtypeStruct((rows, n), F32),
        compiler_params=_cparams(("arbitrary",)),
        name="ada",
    )(cond, w_ada, b_ada.reshape(1, n))


def _pre_body(x_ref, mod_ref, g_ref, wgk_ref, h_ref, gk_ref, *, d):
    x = x_ref[...]
    ms = jnp.mean(x * x, axis=-1, keepdims=True)
    y = x * lax.rsqrt(ms + NORM_EPS) * g_ref[...]
    h = y * (1.0 + mod_ref[:, d:2 * d]) + mod_ref[:, 0:d]
    h_ref[...] = h.astype(BF16)
    h_hi, h_lo = _split_bf16(h, 2)
    w_hi, w_lo = _split_bf16(wgk_ref[...], 2)
    gk_ref[...] = _dot_nt(h_hi, w_hi) + (_dot_nt(h_lo, w_hi) + _dot_nt(h_hi, w_lo))


def _pre(x, mod3, mod_base, tiles_per_mod, norm_g, w_t, gk_col0, r, tm):
    m, d = x.shape
    assert gk_col0 % r == 0 and r % 8 == 0
    return pl.pallas_call(
        functools.partial(_pre_body, d=d),
        grid=(m // tm,),
        in_specs=[
            pl.BlockSpec((tm, d), lambda i: (i, 0)),
            pl.BlockSpec((None, 1, mod3.shape[2]), lambda i: (mod_base + i // tiles_per_mod, 0, 0)),
            pl.BlockSpec((1, d), lambda i: (0, 0)),
            pl.BlockSpec((r, d), lambda i: (gk_col0 // r, 0)),
        ],
        out_specs=[
            pl.BlockSpec((tm, d), lambda i: (i, 0)),
            pl.BlockSpec((tm, r), lambda i: (i, 0)),
        ],
        out_shape=[jax.ShapeDtypeStruct((m, d), BF16), jax.ShapeDtypeStruct((m, r), F32)],
        compiler_params=_cparams(("arbitrary",)),
        name="pre",
    )(x, mod3, norm_g.reshape(1, d), w_t)


def _colcast_body(*refs, shift):
    o_ref = refs[-1]
    a = refs[0][...]
    if shift:
        a = jnp.concatenate([a[shift:, :], refs[1][:shift, :]], axis=0)
    o_ref[...] = a.T.astype(BF16)


def _colcast(w_t, col0, n_cols, bn):
    k = w_t.shape[1]
    base, shift = divmod(col0, bn)
    assert n_cols % bn == 0 and col0 + n_cols <= w_t.shape[0] and shift % 8 == 0
    in_specs = [pl.BlockSpec((bn, k), lambda j: (base + j, 0))]
    if shift:
        in_specs.append(pl.BlockSpec((bn, k), lambda j: (base + j + 1, 0)))
    return pl.pallas_call(
        functools.partial(_colcast_body, shift=shift),
        grid=(n_cols // bn,),
        in_specs=in_specs,
        out_specs=pl.BlockSpec((k, bn), lambda j: (0, j)),
        out_shape=jax.ShapeDtypeStruct((k, n_cols), BF16),
        compiler_params=_cparams(("arbitrary",)),
        name="colcast",
    )(*([w_t] * len(in_specs)))


def _mm_body(a_ref, b_ref, o_ref):
    o_ref[...] = _dot(a_ref[...], b_ref[...]).astype(o_ref.dtype)


def _matmul(a, b, bm, bn):
    m, k = a.shape
    n = b.shape[1]
    while n % bn:
        bn //= 2
    assert bn % LANES == 0
    return pl.pallas_call(
        _mm_body,
        grid=(m // bm, n // bn),
        in_specs=[
            pl.BlockSpec((bm, k), lambda i, j: (i, 0)),
            pl.BlockSpec((k, bn), lambda i, j: (0, j)),
        ],
        out_specs=pl.BlockSpec((bm, bn), lambda i, j: (i, j)),
        out_shape=jax.ShapeDtypeStruct((m, n), BF16),
        compiler_params=_cparams(("arbitrary", "arbitrary")),
        name="in_proj",
    )(a, b)


def _gla_prep(q, k, v, gk_lo, w_up, b_up, reverse):
    c, dk = q.shape
    gk = _dot_f32(gk_lo, w_up) + b_up
    g = (jnp.minimum(gk, 0.0) - jnp.log(1.0 + jnp.exp(-jnp.abs(gk)))) * (1.0 / GATE_NORMALIZER)
    row = lax.broadcasted_iota(I32, (c, c), 0)
    col = lax.broadcasted_iota(I32, (c, c), 1)
    tri = (col >= row) if reverse else (col <= row)
    tri_b = jnp.where(tri, 1.0, 0.0).astype(BF16)
    g1, g2 = _split_bf16(g, 2)
    b = _dot(tri_b, g1) + _dot(tri_b, g2)

    qf = q.astype(F32) * (dk ** -0.5)
    kf = k.astype(F32)
    q_in = (qf * jnp.exp(b)).astype(BF16)

    rowv = lax.broadcasted_iota(I32, (c, 1), 0)
    lhs, rhs = [], []
    for j in range(c // GLA_SUB):
        lo, hi = j * GLA_SUB, (j + 1) * GLA_SUB
        e_row = lo if reverse else hi - 1
        e = b[e_row:e_row + 1, :]
        in_blk = (rowv >= lo) & (rowv < hi)
        active = (rowv < hi) if reverse else (rowv >= lo)
        lhs.append((qf * jnp.exp(jnp.where(active, b - e, NEG_BIG))).astype(BF16))
        rhs.append((kf * jnp.exp(jnp.where(in_blk, e - b, NEG_BIG))).astype(BF16))
    p = _dot_nt(jnp.concatenate(lhs, axis=1), jnp.concatenate(rhs, axis=1))
    p = jnp.where(tri, p, 0.0)
    o_intra = _dot(p.astype(BF16), v)

    l_row = 0 if reverse else c - 1
    bl = b[l_row:l_row + 1, :]
    ks = (kf * jnp.exp(bl - b)).astype(BF16)
    return q_in, ks, jnp.exp(bl), o_intra


def _gla_state_step(prep, v, st_ref):
    q_in, ks, decay, o_intra = prep
    st = st_ref[...]
    o = o_intra + _dot_nt(q_in, st.astype(BF16))
    st_ref[...] = st * decay + _dot_tn(v, ks)
    return o


def _gla_body(*refs, t, unroll, has_init, emit_state):
    q_ref, k_ref, v_ref, og_ref, gk_ref, wup_ref, bup_ref, gn_ref = refs[:8]
    pos = 8
    if has_init:
        s0f_ref, s0b_ref = refs[pos:pos + 2]
        pos += 2
    y_ref = refs[pos]
    pos += 1
    if emit_state:
        sf_ref, sb_ref = refs[pos:pos + 2]
        pos += 2
    o_ref, stf_ref, stb_ref = refs[pos:pos + 3]

    c = GLA_CHUNK
    n = t // c
    if has_init:
        stf_ref[...] = s0f_ref[...].T
        stb_ref[...] = s0b_ref[...].T
    else:
        stf_ref[...] = jnp.zeros_like(stf_ref)
        stb_ref[...] = jnp.zeros_like(stb_ref)

    wup_f, wup_b = wup_ref[0], wup_ref[1]
    bup_f, bup_b = bup_ref[0], bup_ref[1]

    def rows(r):
        return pl.ds(pl.multiple_of(r * c, c), c)

    def make_step(first_touch):
        def step(i, carry):
            jobs = []
            for u in range(unroll):
                cf = i * unroll + u
                jobs.append((rows(cf), wup_f, bup_f, stf_ref, False))
                jobs.append((rows(n - 1 - cf), wup_b, bup_b, stb_ref, True))
            preps = [_gla_prep(q_ref[r, :], k_ref[r, :], v_ref[r, :], gk_ref[r, :], w, bias, rev)
                     for r, w, bias, _, rev in jobs]
            for (r, _, _, st_ref, _), prep in zip(jobs, preps):
                o = _gla_state_step(prep, v_ref[r, :], st_ref)
                if first_touch:
                    o_ref[r, :] = o
                else:
                    o_ref[r, :] += o
            return carry
        return step

    half = n // (2 * unroll)
    lax.fori_loop(0, half, make_step(True), 0)
    lax.fori_loop(half, 2 * half, make_step(False), 0)

    if emit_state:
        sf_ref[...] = stf_ref[...].T
        sb_ref[...] = stb_ref[...].T

    blk = min(t, 256)

    def fin(i, carry):
        r = pl.multiple_of(i * blk, blk)
        o = o_ref[pl.ds(r, blk), :]
        ms = jnp.mean(o * o, axis=-1, keepdims=True)
        og = og_ref[pl.ds(r, blk), :].astype(F32)
        y = o * lax.rsqrt(ms + NORM_EPS) * gn_ref[...] * (og * _sigmoid(og))
        y_ref[pl.ds(r, blk), :] = y.astype(BF16)
        return carry

    lax.fori_loop(0, t // blk, fin, 0)


def _gla(z, gk_lo, w_gk_up, b_gk, gn, n_seq, t, dk, dv, s0f=None, s0b=None, emit_state=False):
    h = GLA_HEADS
    m = n_seq * t
    n_chunks = t // GLA_CHUNK
    unroll = max(1, min(GLA_UNROLL, n_chunks // 2))
    assert t % (2 * unroll * GLA_CHUNK) == 0
    has_init = s0f is not None
    qk_blocks = (h * dk) // dk
    in_specs = [
        pl.BlockSpec((t, dk), lambda s, hh: (s, hh)),
        pl.BlockSpec((t, dk), lambda s, hh: (s, qk_blocks + hh)),
        pl.BlockSpec((t, dv), lambda s, hh: (s, (2 * h * dk) // dv + hh)),
        pl.BlockSpec((t, dv), lambda s, hh: (s, (2 * h * dk) // dv + h + hh)),
        pl.BlockSpec((t, 2 * GK_RANK), lambda s, hh: (s, 0)),
        pl.BlockSpec((2, 2 * GK_RANK, dk), lambda s, hh: (0, 0, hh)),
        pl.BlockSpec((2, 1, dk), lambda s, hh: (0, 0, hh)),
        pl.BlockSpec((1, dv), lambda s, hh: (0, 0)),
    ]
    args = [z, z, z, z, gk_lo, w_gk_up, b_gk.reshape(2, 1, h * dk), gn.reshape(1, dv)]
    if has_init:
        in_specs += [pl.BlockSpec((None, dk, dv), lambda s, hh: (s * h + hh, 0, 0))] * 2
        args += [s0f.reshape(n_seq * h, dk, dv), s0b.reshape(n_seq * h, dk, dv)]
    out_specs = [pl.BlockSpec((t, dv), lambda s, hh: (s, hh))]
    out_shape = [jax.ShapeDtypeStruct((m, h * dv), BF16)]
    if emit_state:
        out_specs += [pl.BlockSpec((None, dk, dv), lambda s, hh: (s * h + hh, 0, 0))] * 2
        out_shape += [jax.ShapeDtypeStruct((n_seq * h, dk, dv), F32)] * 2
    return pl.pallas_call(
        functools.partial(_gla_body, t=t, unroll=unroll, has_init=has_init, emit_state=emit_state),
        grid=(n_seq, h),
        in_specs=in_specs,
        out_specs=out_specs,
        out_shape=out_shape,
        scratch_shapes=[pltpu.VMEM((t, dv), F32), pltpu.VMEM((dv, dk), F32), pltpu.VMEM((dv, dk), F32)],
        compiler_params=_cparams(("arbitrary", "arbitrary")),
        name="gla",
    )(*args)


def _mix_body(yg_ref, cx_ref, cb_ref, cc_ref, ga_ref, gb_ref, cw_ref, wg_ref, wc_ref, o_ref, *, seg):
    tm = yg_ref.shape[0]
    s = cc_ref[...].astype(F32) * cx_ref[...].astype(F32)
    rowv = lax.broadcasted_iota(I32, (tm, 1), 0)
    pos = rowv & (seg - 1)
    prev = jnp.where(pos == 0, 0.0, pltpu.roll(s, 1, axis=0))
    nxt = jnp.where(pos == seg - 1, 0.0, pltpu.roll(s, tm - 1, axis=0))
    conv = prev * cw_ref[0:1, :] + s * cw_ref[1:2, :] + nxt * cw_ref[2:3, :]
    u = (cb_ref[...].astype(F32) * conv).astype(BF16)
    y_gla = _dot(yg_ref[...], wg_ref[...])
    y_conv = _dot(u, wc_ref[...])
    merged = _sigmoid(ga_ref[...].astype(F32)) * y_gla + _sigmoid(gb_ref[...].astype(F32)) * y_conv
    o_ref[...] = merged.astype(BF16)


def _mix(yg, z, conv_w, w_gla_out, w_conv_out, tm, seg, d, col0):
    m = yg.shape[0]
    zspec = lambda c: pl.BlockSpec((tm, d), lambda i: (i, col0 + c))
    const = lambda shape: pl.BlockSpec(shape, lambda i: (0, 0))
    return pl.pallas_call(
        functools.partial(_mix_body, seg=seg),
        grid=(m // tm,),
        in_specs=[pl.BlockSpec((tm, d), lambda i: (i, 0)), zspec(0), zspec(1), zspec(2), zspec(3), zspec(4),
                  const(conv_w.shape), _resident(w_gla_out.shape), _resident(w_conv_out.shape)],
        out_specs=pl.BlockSpec((tm, d), lambda i: (i, 0)),
        out_shape=jax.ShapeDtypeStruct((m, d), BF16),
        compiler_params=_cparams(("arbitrary",)),
        name="mix",
    )(yg, z, z, z, z, z, conv_w, w_gla_out, w_conv_out)


def _route_body(mg_ref, x_ref, mod_ref, wo_ref, g_ref, wr_ref, br_ref, cnt_in_ref,
                x2_ref, hp_ref, route_ref, prob_ref, cnt_ref, carry_ref, *, d, n_exp):
    i = pl.program_id(0)
    tm = x_ref.shape[0]

    @pl.when(i == 0)
    def _():
        carry_ref[...] = cnt_in_ref[...]

    x2 = x_ref[...] + mod_ref[:, 2 * d:3 * d] * _dot(mg_ref[...], wo_ref[...])
    x2_ref[...] = x2
    ms = jnp.mean(x2 * x2, axis=-1, keepdims=True)
    h = x2 * lax.rsqrt(ms + NORM_EPS) * g_ref[...] * (1.0 + mod_ref[:, 4 * d:5 * d]) + mod_ref[:, 3 * d:4 * d]
    hp_ref[...] = h.astype(BF16).reshape(tm, d // LANES, LANES)

    logits = _dot_f32(h, wr_ref[...]) + br_ref[...]
    lane = lax.broadcasted_iota(I32, (tm, n_exp), 1).astype(F32)
    work = logits
    vals, hots = [], []
    for _ in range(TOP_K):
        mx = jnp.max(work, axis=-1, keepdims=True)
        idx = jnp.min(jnp.where(work == mx, lane, float(n_exp)), axis=-1, keepdims=True)
        hot = lane == idx
        vals.append(mx)
        hots.append(hot)
        work = jnp.where(hot, -jnp.inf, work)
    exps = [jnp.exp(v - vals[0]) for v in vals]
    inv = 1.0 / (exps[0] + exps[1] + exps[2] + exps[3])

    cnt = jnp.zeros((tm, n_exp), F32)
    for hot in hots:
        cnt = cnt + jnp.where(hot, 1.0, 0.0)
    r = lax.broadcasted_iota(I32, (tm, tm), 0)
    cc = lax.broadcasted_iota(I32, (tm, tm), 1)
    strict = jnp.where(cc < r, 1.0, 0.0).astype(BF16)
    before = _dot(strict, cnt.astype(BF16)) + carry_ref[...]
    carry_ref[...] = carry_ref[...] + jnp.sum(cnt, axis=0, keepdims=True)

    out_lane = lax.broadcasted_iota(I32, (tm, LANES), 1)
    route = jnp.zeros((tm, LANES), I32)
    probs = jnp.zeros((tm, LANES), F32)
    for k in range(TOP_K):
        e_k = jnp.sum(jnp.where(hots[k], lane, 0.0), axis=-1, keepdims=True).astype(I32)
        rank_k = jnp.sum(jnp.where(hots[k], before, 0.0), axis=-1, keepdims=True).astype(I32)
        route = jnp.where(out_lane == k, e_k, route)
        route = jnp.where(out_lane == TOP_K + k, rank_k, route)
        probs = jnp.where(out_lane == k, exps[k] * inv, probs)
    route_ref[...] = route
    prob_ref[...] = probs

    @pl.when(i == pl.num_programs(0) - 1)
    def _():
        cnt_ref[...] = carry_ref[...]


def _route(merged, x, mod3, mod_base, tiles_per_mod, w_o, norm_g, w_router, b_router, counts_in, tm):
    m, d = x.shape
    n_exp = w_router.shape[1]
    const = lambda shape: pl.BlockSpec(shape, lambda i: (0, 0))
    tile = lambda w: pl.BlockSpec((tm, w), lambda i: (i, 0))
    return pl.pallas_call(
        functools.partial(_route_body, d=d, n_exp=n_exp),
        grid=(m // tm,),
        in_specs=[tile(d), tile(d),
                  pl.BlockSpec((None, 1, mod3.shape[2]), lambda i: (mod_base + i // tiles_per_mod, 0, 0)),
                  _resident(w_o.shape), const((1, d)), const(w_router.shape), const((1, n_exp)),
                  const((1, n_exp))],
        out_specs=[tile(d), pl.BlockSpec((tm, d // LANES, LANES), lambda i: (i, 0, 0)), tile(LANES), tile(LANES),
                   const((1, n_exp))],
        out_shape=[jax.ShapeDtypeStruct((m, d), F32), jax.ShapeDtypeStruct((m, d // LANES, LANES), BF16),
                   jax.ShapeDtypeStruct((m, LANES), I32), jax.ShapeDtypeStruct((m, LANES), F32),
                   jax.ShapeDtypeStruct((1, n_exp), F32)],
        scratch_shapes=[pltpu.VMEM((1, n_exp), F32)],
        compiler_params=_cparams(("arbitrary",)),
        name="route",
    )(merged, x, mod3, w_o, norm_g.reshape(1, d), w_router, b_router.reshape(1, n_exp), counts_in)


def _dispatch_body(pos_ref, zero_ref, hp_a_ref, hp_b_ref, xs_ref, sem, zbuf_ref, *, tb, n_a):
    i = pl.program_id(0)
    n = tb * TOP_K

    @pl.when(i == 0)
    def _():
        zbuf_ref[...] = jnp.zeros_like(zbuf_ref)
        n_gran = zero_ref.shape[0]

        def fill(g, carry):
            @pl.when(zero_ref[g] != 0)
            def _():
                pltpu.make_async_copy(zbuf_ref, xs_ref.at[pl.ds(g * SUB_TILE, SUB_TILE)], sem).start()
            return carry

        def drain(g, carry):
            @pl.when(zero_ref[g] != 0)
            def _():
                pltpu.make_async_copy(zbuf_ref, xs_ref.at[pl.ds(0, SUB_TILE)], sem).wait()
            return carry

        lax.fori_loop(0, n_gran, fill, 0)
        lax.fori_loop(0, n_gran, drain, 0)

    def scatter(hp_ref):
        def issue(r, carry):
            for k in range(TOP_K):
                slot = pos_ref[i * n + r * TOP_K + k]
                pltpu.make_async_copy(hp_ref.at[r], xs_ref.at[slot], sem).start()
            return carry

        lax.fori_loop(0, tb, issue, 0, unroll=4)
        for k in range(TOP_K):
            pltpu.make_async_copy(hp_ref, xs_ref.at[pl.ds(0, tb)], sem).wait()

    @pl.when(i < n_a)
    def _():
        scatter(hp_a_ref)

    @pl.when(i >= n_a)
    def _():
        scatter(hp_b_ref)


def _dispatch(pos_flat, zero_flags, hp_a, hp_b, n_slots, tb):
    nt = hp_a.shape[1]
    n_a, n_b = hp_a.shape[0] // tb, hp_b.shape[0] // tb
    return pl.pallas_call(
        functools.partial(_dispatch_body, tb=tb, n_a=n_a),
        grid_spec=pltpu.PrefetchScalarGridSpec(
            num_scalar_prefetch=2,
            grid=(n_a + n_b,),
            in_specs=[
                pl.BlockSpec((tb, nt, LANES), lambda i, pos, zf: (jnp.minimum(i, n_a - 1), 0, 0)),
                pl.BlockSpec((tb, nt, LANES), lambda i, pos, zf: (jnp.maximum(i - n_a, 0), 0, 0)),
            ],
            out_specs=pl.BlockSpec(memory_space=pl.ANY),
            scratch_shapes=[pltpu.SemaphoreType.DMA(()), pltpu.VMEM((SUB_TILE, nt, LANES), BF16)],
        ),
        out_shape=jax.ShapeDtypeStruct((n_slots, nt, LANES), BF16),
        compiler_params=_cparams(("arbitrary",)),
        name="dispatch",
    )(pos_flat, zero_flags, hp_a, hp_b)


def _experts_body(e_ref, xb_ref, valid_ref, x_ref, w1g_ref, w1u_ref, b1g_ref, b1u_ref, w2_ref, b2_ref,
                  y_ref, act_ref, stage_ref, *, n_fa, n_jb):
    del e_ref, xb_ref
    i = pl.program_id(0)
    s = pl.program_id(1)
    valid = valid_ref[i]
    _, rb, oc = stage_ref.shape
    d = n_jb * oc
    nt = d // LANES
    n_sub = rb // SUB_TILE
    nv = (valid + SUB_TILE - 1) // SUB_TILE

    @pl.when(s == 0)
    def _():
        for t in range(n_sub):
            rows = slice(t * SUB_TILE, (t + 1) * SUB_TILE)

            @pl.when(t < nv)
            def _():
                x2d = x_ref[rows].reshape(SUB_TILE, d)
                for j in range(n_jb):
                    stage_ref[j, rows, :] = x2d[:, j * oc:(j + 1) * oc]

            @pl.when(t >= nv)
            def _():
                y_ref[rows] = jnp.zeros((SUB_TILE, nt, LANES), BF16)

    for v in range(1, n_sub + 1):
        m = v * SUB_TILE

        @pl.when((nv == v) & (s < n_fa))
        def _():
            xs = jnp.concatenate([stage_ref[j, 0:m, :] for j in range(n_jb)], axis=1)
            gate = jnp.minimum(_dot(xs, w1g_ref[...].astype(BF16)) + b1g_ref[...], SWIGLU_LIMIT)
            up = jnp.clip(_dot(xs, w1u_ref[...].astype(BF16)) + b1u_ref[...], -SWIGLU_LIMIT, SWIGLU_LIMIT)
            act = (up + 1.0) * gate * _sigmoid(SWIGLU_ALPHA * gate)
            act_ref[s, 0:m, :] = act.astype(BF16)

        @pl.when((nv == v) & (s >= n_fa))
        def _():
            a = jnp.concatenate([act_ref[f, 0:m, :] for f in range(n_fa)], axis=1)
            out = (_dot(a, w2_ref[...].astype(BF16)) + b2_ref[...]).astype(BF16)
            stage_ref[s - n_fa, 0:m, :] = out

        @pl.when((nv == v) & (s == n_fa + n_jb - 1))
        def _():
            full = jnp.concatenate([stage_ref[j, 0:m, :] for j in range(n_jb)], axis=1)
            y_ref[0:m] = full.reshape(m, nt, LANES)


def _experts(item_e, item_xb, item_valid, xs, w1, b1, w2, b2):
    n_items = item_e.shape[0]
    n_exp, d, ff2 = w1.shape
    ff = ff2 // 2
    nt = d // LANES
    rb = ROW_BLOCK
    n_fa = ff // FF_CHUNK
    n_jb = d // OUT_CHUNK

    def fa(i, s, v):
        return jnp.where(v[i] > 0, jnp.minimum(s, n_fa - 1), n_fa - 1)

    def jb(i, s, v):
        return jnp.where(v[i] > 0, jnp.maximum(s - n_fa, 0), n_jb - 1)

    return pl.pallas_call(
        functools.partial(_experts_body, n_fa=n_fa, n_jb=n_jb),
        grid_spec=pltpu.PrefetchScalarGridSpec(
            num_scalar_prefetch=3,
            grid=(n_items, n_fa + n_jb),
            in_specs=[
                pl.BlockSpec((rb, nt, LANES), lambda i, s, e, xb, v: (xb[i], 0, 0)),
                pl.BlockSpec((None, d, FF_CHUNK), lambda i, s, e, xb, v: (e[i], 0, fa(i, s, v))),
                pl.BlockSpec((None, d, FF_CHUNK), lambda i, s, e, xb, v: (e[i], 0, n_fa + fa(i, s, v))),
                pl.BlockSpec((None, 1, FF_CHUNK), lambda i, s, e, xb, v: (e[i], 0, fa(i, s, v))),
                pl.BlockSpec((None, 1, FF_CHUNK), lambda i, s, e, xb, v: (e[i], 0, n_fa + fa(i, s, v))),
                pl.BlockSpec((None, ff, OUT_CHUNK), lambda i, s, e, xb, v: (e[i], 0, jb(i, s, v))),
                pl.BlockSpec((None, 1, OUT_CHUNK), lambda i, s, e, xb, v: (e[i], 0, jb(i, s, v))),
            ],
            out_specs=pl.BlockSpec((rb, nt, LANES), lambda i, s, e, xb, v: (i, 0, 0)),
            scratch_shapes=[pltpu.VMEM((n_fa, rb, FF_CHUNK), BF16), pltpu.VMEM((n_jb, rb, OUT_CHUNK), BF16)],
        ),
        out_shape=jax.ShapeDtypeStruct((n_items * rb, nt, LANES), BF16),
        compiler_params=_cparams(("arbitrary", "arbitrary")),
        name="experts",
    )(item_e, item_xb, item_valid, xs, w1, w1, b1.reshape(n_exp, 1, ff2), b1.reshape(n_exp, 1, ff2),
      w2, b2.reshape(n_exp, 1, d))


def _combine_body(pos_ref, prob_ref, x2_ref, mod_ref, g_ref, y_hbm, o_ref, buf_ref, sem, *, tb, d):
    i = pl.program_id(0)
    n_steps = pl.num_programs(0)
    n = tb * TOP_K

    def issue_block(blk, slot):
        def issue(r, carry):
            for k in range(TOP_K):
                src = y_hbm.at[pos_ref[blk * n + r * TOP_K + k]]
                pltpu.make_async_copy(src, buf_ref.at[slot, k, r], sem.at[slot]).start()
            return carry

        lax.fori_loop(0, tb, issue, 0, unroll=4)

    @pl.when(i == 0)
    def _():
        issue_block(0, 0)

    @pl.when(i + 1 < n_steps)
    def _():
        issue_block(i + 1, (i + 1) % 2)

    slot = i % 2
    for k in range(TOP_K):
        pltpu.make_async_copy(y_hbm.at[pl.ds(0, tb)], buf_ref.at[slot, k], sem.at[slot]).wait()

    probs = prob_ref[...]
    moe = jnp.zeros((tb, d), F32)
    for k in range(TOP_K):
        moe = moe + probs[:, k:k + 1] * buf_ref[slot, k].reshape(tb, d).astype(F32)
    x = x2_ref[...] + mod_ref[:, 5 * d:6 * d] * moe
    ms = jnp.mean(x * x, axis=-1, keepdims=True)
    o_ref[...] = x * lax.rsqrt(ms + NORM_EPS) * g_ref[...]


def _combine(pos_flat, probs, x2, mod3, mod_base, tiles_per_mod, final_g, y_sorted, tb):
    m, d = x2.shape
    return pl.pallas_call(
        functools.partial(_combine_body, tb=tb, d=d),
        grid_spec=pltpu.PrefetchScalarGridSpec(
            num_scalar_prefetch=1,
            grid=(m // tb,),
            in_specs=[
                pl.BlockSpec((tb, LANES), lambda i, pos: (i, 0)),
                pl.BlockSpec((tb, d), lambda i, pos: (i, 0)),
                pl.BlockSpec((None, 1, mod3.shape[2]), lambda i, pos: (mod_base + i // tiles_per_mod, 0, 0)),
                pl.BlockSpec((1, d), lambda i, pos: (0, 0)),
                pl.BlockSpec(memory_space=pl.ANY),
            ],
            out_specs=pl.BlockSpec((tb, d), lambda i, pos: (i, 0)),
            scratch_shapes=[pltpu.VMEM((2, TOP_K, tb, d // LANES, LANES), BF16), pltpu.SemaphoreType.DMA((2,))],
        ),
        out_shape=jax.ShapeDtypeStruct((m, d), F32),
        compiler_params=_cparams(("arbitrary",)),
        name="combine",
    )(pos_flat, probs, x2, mod3, final_g.reshape(1, d), y_sorted)


def _is_pow2(v):
    return v > 0 and (v & (v - 1)) == 0


def kernel(x_prompt, x_sample, state_gla_fwd, state_gla_bwd, c, c_ctx, norm1_g, w_ada, b_ada, w_in, w_gk_up, b_gk,
           gla_norm_g, w_gla_out, conv_w, w_conv_out, w_o, norm2_g, w_router, b_router, w_exp1, b_exp1, w_exp2,
           b_exp2, final_norm_g):
    batch, seq, d = x_prompt.shape
    dec_batch, dec_seq, _ = x_sample.shape
    depth = norm1_g.shape[0]
    assert depth == 1, "single trunk layer"
    h = GLA_HEADS
    dk_all = w_gk_up.shape[-1]
    dk, dv = dk_all // h, d // h
    n_exp = w_router.shape[-1]
    grid_w = 64
    assert _is_pow2(seq) and _is_pow2(grid_w) and dec_seq % grid_w == 0
    tm = 256
    tm_mix = 512
    assert tm % seq == 0 and tm % grid_w == 0, "conv segments must not straddle token tiles"
    assert tm_mix % tm == 0 and seq * batch % tm_mix == 0 and dec_seq % tm_mix == 0

    c_gk = 2 * dk_all + 2 * d
    w_in_t = jnp.transpose(w_in[0])
    n_rest = w_in_t.shape[0] - c_gk - 2 * GK_RANK
    cast_bn = 512
    w_att = _colcast(w_in_t, 0, c_gk, cast_bn)
    w_rest = _colcast(w_in_t, c_gk + 2 * GK_RANK, n_rest, cast_bn)
    zr = jnp.zeros((GK_RANK, dk_all), F32)
    w_up2 = jnp.stack([jnp.concatenate([w_gk_up[0, 0], zr], axis=0), jnp.concatenate([zr, w_gk_up[0, 1]], axis=0)])
    w_gla_out_b = w_gla_out[0].astype(BF16)
    w_conv_out_b = w_conv_out[0].astype(BF16)
    w_o_b = w_o[0].astype(BF16)

    n_cond = 1 + dec_batch
    cond = jnp.concatenate([c_ctx[None, :], c, jnp.zeros((8 - n_cond % 8, d), F32)], axis=0)
    mod = _ada(cond, w_ada[0], b_ada[0])
    mod3 = mod.reshape(mod.shape[0], 1, mod.shape[1])


    passes = []
    counts = jnp.zeros((1, n_exp), F32)
    for name, x3, n_seq, t, mod_base, seg in (("ctx", x_prompt, batch, seq, 0, seq),
                                              ("lat", x_sample, dec_batch, dec_seq, 1, grid_w)):
        m = n_seq * t
        x = x3.reshape(m, d)
        tiles_per_mod = (m // tm) if name == "ctx" else (t // tm)
        hb, gk_lo = _pre(x, mod3, mod_base, tiles_per_mod, norm1_g[0], w_in_t, c_gk, 2 * GK_RANK, tm)
        z_att = _matmul(hb, w_att, min(m, 2048), 1024)
        z_rest = _matmul(hb, w_rest, min(m, 2048), 1024)
        if name == "ctx":
            yg, s_f, s_b = _gla(z_att, gk_lo, w_up2, b_gk[0], gla_norm_g[0], n_seq, t, dk, dv, emit_state=True)
        else:
            (yg,) = _gla(z_att, gk_lo, w_up2, b_gk[0], gla_norm_g[0], n_seq, t, dk, dv,
                         s0f=state_gla_fwd[:, 0], s0b=state_gla_bwd[:, 0])
            s_f = s_b = None
        merged = _mix(yg, z_rest, conv_w[0], w_gla_out_b, w_conv_out_b, tm_mix, seg, d, 0)
        x2, hp, route, probs, counts = _route(merged, x, mod3, mod_base, tiles_per_mod * tm // tm_mix, w_o_b,
                                              norm2_g[0], w_router[0], b_router[0], counts, tm_mix)
        passes.append(dict(x2=x2, hp=hp, route=route, probs=probs, s_f=s_f, s_b=s_b, mod_base=mod_base,
                           tiles_per_mod=tiles_per_mod, shape=x3.shape))

    n_tok = sum(p["x2"].shape[0] for p in passes)
    n_items = (n_tok * TOP_K) // ROW_BLOCK + n_exp
    cnt = counts[0].astype(I32)
    nblk = (cnt + ROW_BLOCK - 1) // ROW_BLOCK
    blk_end = jnp.cumsum(nblk)
    blk0 = blk_end - nblk
    total = blk_end[-1]
    item = jnp.arange(n_items, dtype=I32)
    owner = lambda blk: jnp.sum((blk_end[None, :] <= blk[:, None]).astype(I32), axis=1)
    live = item < total
    item_e = jnp.where(live, jnp.minimum(owner(item), n_exp - 1), owner(total[None] - 1)[0])
    item_xb = jnp.where(live, item, total - 1)
    item_valid = jnp.where(live, jnp.clip(cnt[item_e] - (item - blk0[item_e]) * ROW_BLOCK, 0, ROW_BLOCK), 0)
    slot0 = blk0 * ROW_BLOCK

    for p in passes:
        er = p["route"][:, 0:2 * TOP_K].T
        pos_t = slot0[er[0:TOP_K]] + er[TOP_K:2 * TOP_K]
        p["pos"] = pos_t.T.reshape(-1)
    sub_per_blk = ROW_BLOCK // SUB_TILE
    gran = jnp.arange(n_items * sub_per_blk, dtype=I32)
    covered = (gran % sub_per_blk + 1) * SUB_TILE <= item_valid[gran // sub_per_blk]
    zero_flags = jnp.where(covered, 0, 1).astype(I32)
    pos_all = jnp.concatenate([p["pos"] for p in passes])
    xs = _dispatch(pos_all, zero_flags, passes[0]["hp"], passes[1]["hp"], n_items * ROW_BLOCK, tm_mix)

    y_sorted = _experts(item_e, item_xb, item_valid, xs, w_exp1[0], b_exp1[0], w_exp2[0], b_exp2[0])

    outs = []
    for p in passes:
        y = _combine(p["pos"], p["probs"], p["x2"], mod3, p["mod_base"], p["tiles_per_mod"], final_norm_g,
                     y_sorted, tm)
        outs.append(y.reshape(p["shape"]))

    st_shape = (batch, depth, h, dk, dv)
    return (outs[0], outs[1], passes[0]["s_f"].reshape(st_shape), passes[0]["s_b"].reshape(st_shape))
```
